```python
import jax
import jax.numpy as jnp
from jax import lax
import numpy as np

D_MODEL = 2048
BATCH = 4
SEQ = 8192
DEPTH = 1
DEC_BATCH = 8
DEC_SEQ = 2048
PAST_LEN = 128

HEAD_DIM = 128
N_HEADS_A = D_MODEL // 256
N_KV_A = max(1, N_HEADS_A // 4)
GROUP_A = N_HEADS_A // N_KV_A
N_HEADS_B = D_MODEL // 256
N_KV_B = max(1, N_HEADS_B // 4)
GROUP_B = N_HEADS_B // N_KV_B
Q_WIDTH_A = N_HEADS_A * HEAD_DIM
KV_WIDTH_A = N_KV_A * HEAD_DIM
Q_WIDTH_B = N_HEADS_B * HEAD_DIM
KV_WIDTH_B = N_KV_B * HEAD_DIM
IN_COLS = Q_WIDTH_A + 2 * KV_WIDTH_A + Q_WIDTH_B + 2 * KV_WIDTH_B + 2 * D_MODEL
D_FF = ((8 * D_MODEL // 3 + 255) // 256) * 256
CONV_WIDTH = 3
BLOCK_Q = 128
WINDOW = 128
GRID_W = 64
ROPE_THETA = 10000.0
EPS = 1e-6

kernel_name = 'hybrid_gated_axial_window_encoder'


def rms_norm(x, gain):
    xf = x.astype(jnp.float32)
    xf = xf * lax.rsqrt(jnp.mean(xf * xf, axis=-1, keepdims=True) + EPS)
    return (xf * gain.astype(jnp.float32)).astype(x.dtype)


def rope_cos_sin(pos, dim):
    inv_freq = ROPE_THETA ** (-jnp.arange(0, dim, 2, dtype=jnp.float32) / dim)
    ang = pos.astype(jnp.float32)[:, None] * inv_freq[None, :]
    ang = jnp.concatenate([ang, ang], axis=-1)
    return jnp.cos(ang), jnp.sin(ang)


def apply_rope(x, cos, sin):
    half = x.shape[-1] // 2
    xf = x.astype(jnp.float32)
    rot = jnp.concatenate([-xf[..., half:], xf[..., :half]], axis=-1)
    return (xf * cos[None, :, None, :] + rot * sin[None, :, None, :]).astype(x.dtype)


def apply_axial_rope(x, cos_r, sin_r, cos_c, sin_c):
    h = HEAD_DIM // 2
    return jnp.concatenate([apply_rope(x[..., :h], cos_r, sin_r),
                            apply_rope(x[..., h:], cos_c, sin_c)], axis=-1)


def dense_attention(q, k, v):
    B, S, Hkv, G, Dh = q.shape
    nb = S // BLOCK_Q
    qb = q.reshape(B, nb, BLOCK_Q, Hkv, G, Dh).transpose(1, 0, 2, 3, 4, 5)
    scale = Dh ** -0.5

    def one_block(qi):
        s = jnp.einsum('bqhgd,bkhd->bhgqk', qi, k, preferred_element_type=jnp.float32) * scale
        p = jax.nn.softmax(s, axis=-1)
        return jnp.einsum('bhgqk,bkhd->bqhgd', p.astype(v.dtype), v)

    o = lax.map(one_block, qb)
    return o.transpose(1, 0, 2, 3, 4, 5).reshape(B, S, Hkv * G * Dh)


def windowed_attention(q, k, v, sink):
    B, S, Hkv, G, Dh = q.shape
    nb = S // BLOCK_Q
    span = BLOCK_Q + 2 * WINDOW
    kp = jnp.pad(k, ((0, 0), (WINDOW, WINDOW), (0, 0), (0, 0)))
    vp = jnp.pad(v, ((0, 0), (WINDOW, WINDOW), (0, 0), (0, 0)))
    a = jnp.arange(BLOCK_Q)[:, None]
    b = jnp.arange(span)[None, :]
    band = (b >= a) & (b <= a + 2 * WINDOW)
    qb = q.reshape(B, nb, BLOCK_Q, Hkv, G, Dh).transpose(1, 0, 2, 3, 4, 5)
    sink_l = sink.astype(jnp.float32).reshape(Hkv, G)[None, :, :, None, None]
    scale = Dh ** -0.5

    def one_block(args):
        i, qi = args
        start = i * BLOCK_Q
        ki = lax.dynamic_slice_in_dim(kp, start, span, axis=1)
        vi = lax.dynamic_slice_in_dim(vp, start, span, axis=1)
        key_pos = start - WINDOW + jnp.arange(span)
        valid = band & ((key_pos >= 0) & (key_pos < S))[None, :]
        s = jnp.einsum('bqhgd,bkhd->bhgqk', qi, ki, preferred_element_type=jnp.float32) * scale
        s = jnp.where(valid, s, -jnp.inf)
        m = jnp.maximum(jnp.max(s, axis=-1, keepdims=True), sink_l)
        p = jnp.exp(s - m)
        p = p / (jnp.sum(p, axis=-1, keepdims=True) + jnp.exp(sink_l - m))
        return jnp.einsum('bhgqk,bkhd->bqhgd', p.astype(vi.dtype), vi)

    o = lax.map(one_block, (jnp.arange(nb), qb))
    return o.transpose(1, 0, 2, 3, 4, 5).reshape(B, S, Hkv * G * Dh)


def token_mixer(h, w_in, q_norm_a, k_norm_a, sink_b, w_branch_a, w_branch_b, w_out):
    B, S, _ = h.shape
    rows = S // GRID_W
    t = jnp.arange(S, dtype=jnp.int32)
    row_pos = jnp.repeat(jnp.arange(rows, dtype=jnp.int32), GRID_W)
    col_pos = jnp.tile(jnp.arange(GRID_W, dtype=jnp.int32), rows)
    cos_r, sin_r = rope_cos_sin(row_pos, HEAD_DIM // 2)
    cos_c, sin_c = rope_cos_sin(col_pos, HEAD_DIM // 2)
    cos_t, sin_t = rope_cos_sin(t, HEAD_DIM)

    proj = h @ w_in
    sizes = (Q_WIDTH_A, KV_WIDTH_A, KV_WIDTH_A, Q_WIDTH_B, KV_WIDTH_B, KV_WIDTH_B, D_MODEL, D_MODEL)
    offsets = [sum(sizes[:i]) for i in range(1, len(sizes))]
    qa, ka, va, qb, kb, vb, gate_a, gate_b = jnp.split(proj, offsets, axis=-1)

    qa = rms_norm(qa.reshape(B, S, N_HEADS_A, HEAD_DIM), q_norm_a)
    ka = rms_norm(ka.reshape(B, S, N_KV_A, HEAD_DIM), k_norm_a)
    qa = apply_axial_rope(qa, cos_r, sin_r, cos_c, sin_c)
    ka = apply_axial_rope(ka, cos_r, sin_r, cos_c, sin_c)
    va = va.reshape(B, S, N_KV_A, HEAD_DIM)
    o_a = dense_attention(qa.reshape(B, S, N_KV_A, GROUP_A, HEAD_DIM), ka, va)

    qb = apply_rope(qb.reshape(B, S, N_HEADS_B, HEAD_DIM), cos_t, sin_t)
    kb = apply_rope(kb.reshape(B, S, N_KV_B, HEAD_DIM), cos_t, sin_t)
    vb = vb.reshape(B, S, N_KV_B, HEAD_DIM)
    o_b = windowed_attention(qb.reshape(B, S, N_KV_B, GROUP_B, HEAD_DIM), kb, vb, sink_b)

    merged = jax.nn.sigmoid(gate_a) * (o_a @ w_branch_a) + jax.nn.sigmoid(gate_b) * (o_b @ w_branch_b)
    return merged @ w_out


def channel_mixer(h, w_up, conv_w, conv_b, w_down):
    S = h.shape[1]
    up = h @ w_up
    pad = CONV_WIDTH // 2
    up_p = jnp.pad(up, ((0, 0), (pad, pad), (0, 0)))
    conv = conv_b
    for j in range(CONV_WIDTH):
        conv = conv + up_p[:, j:j + S] * conv_w[j]
    a, b = jnp.split(conv, 2, axis=-1)
    return (jax.nn.gelu(a, approximate=True) * b) @ w_down


def run_trunk(x, norm_pre_mix, w_in, q_norm_a, k_norm_a, sink_b, w_branch_a, w_branch_b, w_out,
              norm_post_mix, norm_pre_ffn, w_up, conv_w, conv_b, w_down, norm_post_ffn):
    for l in range(DEPTH):
        h = rms_norm(x, norm_pre_mix[l])
        mix = token_mixer(h, w_in[l], q_norm_a[l], k_norm_a[l], sink_b[l],
                          w_branch_a[l], w_branch_b[l], w_out[l])
        x = x + rms_norm(mix, norm_post_mix[l])
        h = rms_norm(x, norm_pre_ffn[l])
        ffn = channel_mixer(h, w_up[l], conv_w[l], conv_b[l], w_down[l])
        x = x + rms_norm(ffn, norm_post_ffn[l])
    return x


def setup_inputs(seed: int = 0) -> dict:
    key = jax.random.key(seed)
    ks = jax.random.split(key, 17)
    f32 = jnp.float32

    def dense(k, shape, fan_in):
        return jax.random.normal(k, shape, f32) * fan_in ** -0.5

    def gain(k, n):
        return 1.0 + 0.05 * jax.random.normal(k, (DEPTH, n), f32)

    return {
        'x_prompt': jax.random.normal(ks[0], (BATCH, SEQ, D_MODEL), f32),
        'x_sample': jax.random.normal(ks[1], (DEC_BATCH, DEC_SEQ, D_MODEL), f32),
        'norm_pre_mix': gain(ks[2], D_MODEL),
        'w_in': dense(ks[3], (DEPTH, D_MODEL, IN_COLS), D_MODEL),
        'q_norm_a': gain(ks[4], HEAD_DIM),
        'k_norm_a': gain(ks[5], HEAD_DIM),
        'sink_b': 0.5 * jax.random.normal(ks[6], (DEPTH, N_HEADS_B), f32),
        'w_branch_a': dense(ks[7], (DEPTH, Q_WIDTH_A, D_MODEL), Q_WIDTH_A),
        'w_branch_b': dense(ks[8], (DEPTH, Q_WIDTH_B, D_MODEL), Q_WIDTH_B),
        'w_out': dense(ks[9], (DEPTH, D_MODEL, D_MODEL), D_MODEL),
        'norm_post_mix': gain(ks[10], D_MODEL),
        'norm_pre_ffn': gain(ks[11], D_MODEL),
        'w_up': dense(ks[12], (DEPTH, D_MODEL, 2 * D_FF), D_MODEL),
        'conv_w': dense(ks[13], (DEPTH, CONV_WIDTH, 2 * D_FF), CONV_WIDTH),
        'conv_b': 0.02 * jax.random.normal(ks[14], (DEPTH, 2 * D_FF), f32),
        'w_down': dense(ks[15], (DEPTH, D_FF, D_MODEL), D_FF),
        'norm_post_ffn': gain(ks[16], D_MODEL),
    }


def reference(x_prompt, x_sample, norm_pre_mix, w_in, q_norm_a, k_norm_a, sink_b, w_branch_a,
              w_branch_b, w_out, norm_post_mix, norm_pre_ffn, w_up, conv_w, conv_b, w_down,
              norm_post_ffn):
    y_prompt = run_trunk(x_prompt, norm_pre_mix, w_in, q_norm_a, k_norm_a, sink_b, w_branch_a,
                         w_branch_b, w_out, norm_post_mix, norm_pre_ffn, w_up, conv_w, conv_b,
                         w_down, norm_post_ffn)
    y_sample = run_trunk(x_sample, norm_pre_mix, w_in, q_norm_a, k_norm_a, sink_b, w_branch_a,
                         w_branch_b, w_out, norm_post_mix, norm_pre_ffn, w_up, conv_w, conv_b,
                         w_down, norm_post_ffn)
    return (y_prompt, y_sample)
```

```python
import functools
import math

import jax
import jax.numpy as jnp
from jax import lax
from jax.experimental import pallas as pl
from jax.experimental.pallas import tpu as pltpu

HEAD_DIM = 128
WINDOW = 128
GRID_W = 64
ROPE_THETA = 10000.0
EPS = 1e-6
LANES = 128
BF16_SUBLANES = 16
VMEM_LIMIT = 56 * 1024 * 1024

F32 = jnp.float32
BF16 = jnp.bfloat16


def _pick(n, candidates):
    for c in candidates:
        if n % c == 0:
            return c
    raise ValueError(f"no tile in {candidates} divides {n}")


def _rms(x, gain):
    r = lax.rsqrt(jnp.mean(x * x, axis=-1, keepdims=True) + EPS)
    return x * r * gain


def _params(sem):
    return pltpu.CompilerParams(dimension_semantics=sem, vmem_limit_bytes=VMEM_LIMIT)


def _runs(js):
    runs, lo, prev = [], js[0], js[0]
    for j in js[1:]:
        if j != prev + 1:
            runs.append((lo, prev))
            lo = j
        prev = j
    runs.append((lo, prev))
    return runs


def _in_proj_kernel(groups, x_ref, g_ref, w_ref, gq_ref, gk_ref, cosa_ref, sa1_ref, sa2_ref,
                    cost_ref, sint_ref, o_ref, hs_ref):
    j = pl.program_id(1)

    @pl.when(j == 0)
    def _():
        hs_ref[...] = _rms(x_ref[...], g_ref[...]).astype(BF16)

    acc = jnp.dot(hs_ref[...], w_ref[...], preferred_element_type=F32)

    def axial(h):
        return (h * cosa_ref[...] + pltpu.roll(h, 96, 1) * sa1_ref[...]
                + pltpu.roll(h, 32, 1) * sa2_ref[...])

    def rope1d(h):
        return h * cost_ref[...] + pltpu.roll(h, 64, 1) * sint_ref[...]

    def epilogue(kind, h):
        if kind == "qa":
            return axial(_rms(h, gq_ref[...]))
        if kind == "ka":
            return axial(_rms(h, gk_ref[...]))
        if kind == "qb":
            return rope1d(h) * (HEAD_DIM ** -0.5)
        if kind == "kb":
            return rope1d(h)
        if kind == "gate":
            return jax.nn.sigmoid(h)
        return h

    for pattern, js in groups:
        cond = None
        for lo, hi in _runs(js):
            c = (j == lo) if lo == hi else ((j >= lo) & (j <= hi))
            cond = c if cond is None else (cond | c)

        @pl.when(cond)
        def _(pattern=pattern):
            for c, kind in enumerate(pattern):
                sl = slice(c * LANES, (c + 1) * LANES)
                o_ref[:, sl] = epilogue(kind, acc[:, sl]).astype(o_ref.dtype)


def _in_proj(x, gain, w, gq, gk, tabs, seq, seg_kinds):
    T, D = x.shape
    N = w.shape[1]
    tm = _pick(seq, (1024, 512, 256, 128))
    tn = _pick(N, (512, 256, 128))
    chunk_kind = []
    for kind, n in seg_kinds:
        chunk_kind += [kind] * (n // LANES)
    per = tn // LANES
    patterns = {}
    for j in range(N // tn):
        patterns.setdefault(tuple(chunk_kind[j * per:(j + 1) * per]), []).append(j)
    groups = tuple(patterns.items())
    seq_tiles = seq // tm
    tab_spec = pl.BlockSpec((tm, LANES), lambda i, j: (i % seq_tiles, 0))
    vec = lambda n: pl.BlockSpec((1, n), lambda i, j: (0, 0))
    return pl.pallas_call(
        functools.partial(_in_proj_kernel, groups),
        grid=(T // tm, N // tn),
        in_specs=[pl.BlockSpec((tm, D), lambda i, j: (i, 0)), vec(D),
                  pl.BlockSpec((D, tn), lambda i, j: (0, j)), vec(LANES), vec(LANES),
                  tab_spec, tab_spec, tab_spec, tab_spec, tab_spec],
        out_specs=pl.BlockSpec((tm, tn), lambda i, j: (i, j)),
        out_shape=jax.ShapeDtypeStruct((T, N), BF16),
        scratch_shapes=[pltpu.VMEM((tm, D), BF16)],
        compiler_params=_params(("parallel", "arbitrary")),
        name="in_proj",
    )(x, gain, w, gq, gk, *tabs)


def _stack_heads(q, group):
    return jnp.concatenate([q[:, g * HEAD_DIM:(g + 1) * HEAD_DIM] for g in range(group)], axis=0)


def _unstack_heads(o_ref, o, group, tq):
    for g in range(group):
        o_ref[:, g * HEAD_DIM:(g + 1) * HEAD_DIM] = o[g * tq:(g + 1) * tq].astype(o_ref.dtype)


def _qk(q, k):
    return lax.dot_general(q, k, (((1,), (1,)), ((), ())), preferred_element_type=F32)


def _attn_dense_kernel(group, tq, tk, q_ref, k_ref, v_ref, o_ref):
    rows = group * tq
    q = _stack_heads(q_ref[...], group)
    n_chunks = k_ref.shape[0] // tk

    def body(c, carry):
        m, l, acc = carry
        off = pl.multiple_of(c * tk, tk)
        k = k_ref[pl.ds(off, tk), :]
        v = v_ref[pl.ds(off, tk), :]
        s = _qk(q, k)
        m_new = jnp.maximum(m, jnp.max(s, axis=-1, keepdims=True))
        alpha = jnp.exp(m - m_new)
        p = jnp.exp(s - m_new)
        l = alpha * l + jnp.sum(p, axis=-1, keepdims=True)
        acc = alpha * acc + jnp.dot(p.astype(BF16), v, preferred_element_type=F32)
        return m_new, l, acc

    init = (jnp.full((rows, 1), -jnp.inf, F32), jnp.zeros((rows, 1), F32),
            jnp.zeros((rows, HEAD_DIM), F32))
    _, l, acc = lax.fori_loop(0, n_chunks, body, init)
    _unstack_heads(o_ref, acc / l, group, tq)


def _attn_dense(proj, batch, seq, n_kv, group, q_col, k_col, v_col):
    T = proj.shape[0]
    tq = _pick(seq, (256, 128))
    tk = _pick(seq, (512, 256, 128))
    nq = seq // tq
    qw = group * HEAD_DIM
    return pl.pallas_call(
        functools.partial(_attn_dense_kernel, group, tq, tk),
        grid=(batch, n_kv, nq),
        in_specs=[pl.BlockSpec((tq, qw), lambda b, h, i: (b * nq + i, q_col // qw + h)),
                  pl.BlockSpec((seq, HEAD_DIM), lambda b, h, i: (b, k_col // HEAD_DIM + h)),
                  pl.BlockSpec((seq, HEAD_DIM), lambda b, h, i: (b, v_col // HEAD_DIM + h))],
        out_specs=pl.BlockSpec((tq, qw), lambda b, h, i: (b * nq + i, h)),
        out_shape=jax.ShapeDtypeStruct((T, n_kv * qw), BF16),
        compiler_params=_params(("parallel", "parallel", "arbitrary")),
        name="attn_dense",
    )(proj, proj, proj)


def _attn_win_kernel(group, tq, span, sink_ref, q_ref, k_ref, v_ref, o_ref):
    h = pl.program_id(1)
    qi = pl.program_id(2)
    seq = k_ref.shape[0]
    start = pl.multiple_of(jnp.clip(qi * tq - WINDOW, 0, seq - span), LANES)
    k = k_ref[pl.ds(start, span), :]
    v = v_ref[pl.ds(start, span), :]
    q = _stack_heads(q_ref[...], group)
    s = _qk(q, k)
    q_pos = qi * tq + lax.broadcasted_iota(jnp.int32, (tq, span), 0)
    k_pos = start + lax.broadcasted_iota(jnp.int32, (tq, span), 1)
    valid = jnp.abs(q_pos - k_pos) <= WINDOW
    valid = jnp.concatenate([valid] * group, axis=0)
    s = jnp.where(valid, s, -jnp.inf)
    sink = jnp.concatenate([jnp.full((tq, 1), sink_ref[h * group + g], F32) for g in range(group)],
                           axis=0)
    m = jnp.maximum(jnp.max(s, axis=-1, keepdims=True), sink)
    p = jnp.exp(s - m)
    denom = jnp.sum(p, axis=-1, keepdims=True) + jnp.exp(sink - m)
    o = jnp.dot(p.astype(BF16), v, preferred_element_type=F32) / denom
    _unstack_heads(o_ref, o, group, tq)


def _attn_win(proj, sink, batch, seq, n_kv, group, q_col, k_col, v_col):
    T = proj.shape[0]
    tq = _pick(seq, (128,))
    span = min(tq + 2 * WINDOW, seq)
    nq = seq // tq
    qw = group * HEAD_DIM
    return pl.pallas_call(
        functools.partial(_attn_win_kernel, group, tq, span),
        grid=(batch, n_kv, nq),
        in_specs=[pl.BlockSpec(memory_space=pltpu.SMEM),
                  pl.BlockSpec((tq, qw), lambda b, h, i: (b * nq + i, q_col // qw + h)),
                  pl.BlockSpec((seq, HEAD_DIM), lambda b, h, i: (b, k_col // HEAD_DIM + h)),
                  pl.BlockSpec((seq, HEAD_DIM), lambda b, h, i: (b, v_col // HEAD_DIM + h))],
        out_specs=pl.BlockSpec((tq, qw), lambda b, h, i: (b * nq + i, h)),
        out_shape=jax.ShapeDtypeStruct((T, n_kv * qw), BF16),
        compiler_params=_params(("parallel", "parallel", "arbitrary")),
        name="attn_win",
    )(sink, proj, proj, proj)


def _merge_kernel(n_gc, gw, *refs):
    oa_ref, ob_ref = refs[0], refs[1]
    ga_refs = refs[2:2 + n_gc]
    gb_refs = refs[2 + n_gc:2 + 2 * n_gc]
    x_ref, wa_ref, wb_ref, wo_ref, gpost_ref, gffn_ref, x1_ref, h2_ref = refs[2 + 2 * n_gc:]
    oa = oa_ref[...]
    ob = ob_ref[...]
    parts = []
    for c in range(n_gc):
        sl = slice(c * gw, (c + 1) * gw)
        a = jnp.dot(oa, wa_ref[:, sl], preferred_element_type=F32)
        b = jnp.dot(ob, wb_ref[:, sl], preferred_element_type=F32)
        merged = ga_refs[c][...].astype(F32) * a + gb_refs[c][...].astype(F32) * b
        parts.append(merged.astype(BF16))
    merged = jnp.concatenate(parts, axis=1)
    mix = jnp.dot(merged, wo_ref[...], preferred_element_type=F32)
    x1 = x_ref[...] + _rms(mix, gpost_ref[...])
    x1_ref[...] = x1
    h2_ref[...] = _rms(x1, gffn_ref[...]).astype(BF16)


def _resident(shape):
    return pl.BlockSpec(shape, lambda i: (0, 0), pipeline_mode=pl.Buffered(1))


def _merge_out(oa, ob, proj, x, wa, wb, wo, gpost, gffn, gate_col):
    T, D = x.shape
    qa_w, qb_w = oa.shape[1], ob.shape[1]
    tm = _pick(T, (256, 128))
    gw = math.gcd(gate_col, D)
    n_gc = D // gw
    gate_specs = [pl.BlockSpec((tm, gw), functools.partial(lambda i, c: (i, c), c=gate_col // gw + c))
                  for c in range(2 * n_gc)]
    vec = pl.BlockSpec((1, D), lambda i: (0, 0))
    row = lambda n: pl.BlockSpec((tm, n), lambda i: (i, 0))
    return pl.pallas_call(
        functools.partial(_merge_kernel, n_gc, gw),
        grid=(T // tm,),
        in_specs=[row(qa_w), row(qb_w), *gate_specs, row(D),
                  _resident(wa.shape), _resident(wb.shape), _resident(wo.shape), vec, vec],
        out_specs=[row(D), row(D)],
        out_shape=[jax.ShapeDtypeStruct((T, D), F32), jax.ShapeDtypeStruct((T, D), BF16)],
        compiler_params=_params(("parallel",)),
        name="merge_out",
    )(oa, ob, *([proj] * (2 * n_gc)), x, wa, wb, wo, gpost, gffn)


def _ffn_up_kernel(tm, seq_tiles, hm_ref, hp_ref, hn_ref, wa_ref, wb_ref, cwa_ref, cwb_ref,
                   cba_ref, cbb_ref, o_ref, hs_ref):
    i = pl.program_id(0)
    j = pl.program_id(1)
    halo = BF16_SUBLANES

    @pl.when(j == 0)
    def _():
        first = (i % seq_tiles) == 0
        last = (i % seq_tiles) == seq_tiles - 1
        hp = hp_ref[...]
        hn = hn_ref[...]
        hs_ref[0:halo, :] = jnp.where(first, jnp.zeros_like(hp), hp)
        hs_ref[halo:halo + tm, :] = hm_ref[...]
        hs_ref[halo + tm:, :] = jnp.where(last, jnp.zeros_like(hn), hn)

    hs = hs_ref[...]

    def conv_half(w_ref, cw_ref, cb_ref):
        u = jnp.dot(hs, w_ref[...], preferred_element_type=F32)
        cw = cw_ref[...]
        c = cb_ref[...] + u[halo - 1:halo - 1 + tm] * cw[0:1]
        c = c + u[halo:halo + tm] * cw[1:2]
        return c + u[halo + 1:halo + 1 + tm] * cw[2:3]

    a = conv_half(wa_ref, cwa_ref, cba_ref)
    b = conv_half(wb_ref, cwb_ref, cbb_ref)
    o_ref[...] = (jax.nn.gelu(a, approximate=True) * b).astype(o_ref.dtype)


def _ffn_up(h2, w_up, conv_w, conv_b, seq):
    T, D = h2.shape
    F = w_up.shape[1] // 2
    tm = _pick(seq, (1024, 512, 256, 128))
    tn = _pick(F, (512, 256, 128))
    nb = F // tn
    halo = BF16_SUBLANES
    per = tm // halo
    n_halo_blocks = T // halo
    seq_tiles = seq // tm
    return pl.pallas_call(
        functools.partial(_ffn_up_kernel, tm, seq_tiles),
        grid=(T // tm, nb),
        in_specs=[pl.BlockSpec((tm, D), lambda i, j: (i, 0)),
                  pl.BlockSpec((halo, D), lambda i, j: (jnp.maximum(i * per - 1, 0), 0)),
                  pl.BlockSpec((halo, D), lambda i, j: (jnp.minimum((i + 1) * per, n_halo_blocks - 1), 0)),
                  pl.BlockSpec((D, tn), lambda i, j: (0, j)),
                  pl.BlockSpec((D, tn), lambda i, j: (0, nb + j)),
                  pl.BlockSpec((3, tn), lambda i, j: (0, j)),
                  pl.BlockSpec((3, tn), lambda i, j: (0, nb + j)),
                  pl.BlockSpec((1, tn), lambda i, j: (0, j)),
                  pl.BlockSpec((1, tn), lambda i, j: (0, nb + j))],
        out_specs=pl.BlockSpec((tm, tn), lambda i, j: (i, j)),
        out_shape=jax.ShapeDtypeStruct((T, F), BF16),
        scratch_shapes=[pltpu.VMEM((tm + 2 * halo, D), BF16)],
        compiler_params=_params(("parallel", "arbitrary")),
        name="ffn_up",
    )(h2, h2, h2, w_up, w_up, conv_w, conv_w, conv_b, conv_b)


def _ffn_down_kernel(g_ref, w_ref, x1_ref, gain_ref, o_ref):
    f = jnp.dot(g_ref[...], w_ref[...], preferred_element_type=F32)
    o_ref[...] = x1_ref[...] + _rms(f, gain_ref[...])


def _ffn_down(g, w_down, x1, gain):
    T, F = g.shape
    D = x1.shape[1]
    tm = _pick(T, (256, 128))
    return pl.pallas_call(
        _ffn_down_kernel,
        grid=(T // tm,),
        in_specs=[pl.BlockSpec((tm, F), lambda i: (i, 0)), _resident(w_down.shape),
                  pl.BlockSpec((tm, D), lambda i: (i, 0)), pl.BlockSpec((1, D), lambda i: (0, 0))],
        out_specs=pl.BlockSpec((tm, D), lambda i: (i, 0)),
        out_shape=jax.ShapeDtypeStruct((T, D), F32),
        compiler_params=_params(("parallel",)),
        name="ffn_down",
    )(g, w_down, x1, gain)


def _rope_angles(pos, dim):
    inv_freq = ROPE_THETA ** (-jnp.arange(0, dim, 2, dtype=F32) / dim)
    ang = pos.astype(F32)[:, None] * inv_freq[None, :]
    return jnp.concatenate([ang, ang], axis=-1)


def _rope_tables(seq):
    t = jnp.arange(seq, dtype=jnp.int32)
    half = HEAD_DIM // 2
    ang_a = jnp.concatenate([_rope_angles(t // GRID_W, half), _rope_angles(t % GRID_W, half)], axis=-1)
    lane = jnp.arange(HEAD_DIM)[None, :]
    low = (lane % half) < (half // 2)
    cos_a, sin_a = jnp.cos(ang_a), jnp.sin(ang_a)
    sa1 = jnp.where(low, -sin_a, 0.0)
    sa2 = jnp.where(low, 0.0, sin_a)
    ang_t = _rope_angles(t, HEAD_DIM)
    sin_t = jnp.where(lane < half, -jnp.sin(ang_t), jnp.sin(ang_t))
    return cos_a, sa1, sa2, jnp.cos(ang_t), sin_t


def _trunk(x, p):
    B, S, D = x.shape
    T = B * S
    qa_w = p["wa"].shape[0]
    qb_w = p["wb"].shape[0]
    n_cols = p["w_in"].shape[1]
    kv_w = (n_cols - 2 * D - qa_w - qb_w) // 4
    n_kv = kv_w // HEAD_DIM
    group_a = qa_w // kv_w
    group_b = qb_w // kv_w
    seg = (("qa", qa_w), ("ka", kv_w), ("va", kv_w), ("qb", qb_w), ("kb", kv_w), ("vb", kv_w),
           ("gate", 2 * D))
    col = {}
    off = 0
    for kind, n in seg:
        col[kind] = off
        off += n

    x2 = x.reshape(T, D)
    proj = _in_proj(x2, p["g_pre_mix"], p["w_in"], p["gq"], p["gk"], _rope_tables(S), S, seg)
    oa = _attn_dense(proj, B, S, n_kv, group_a, col["qa"], col["ka"], col["va"])
    ob = _attn_win(proj, p["sink"], B, S, n_kv, group_b, col["qb"], col["kb"], col["vb"])
    x1, h2 = _merge_out(oa, ob, proj, x2, p["wa"], p["wb"], p["wo"], p["g_post_mix"], p["g_pre_ffn"],
                        col["gate"])
    g = _ffn_up(h2, p["w_up"], p["conv_w"], p["conv_b"], S)
    y = _ffn_down(g, p["w_down"], x1, p["g_post_ffn"])
    return y.reshape(B, S, D)


def kernel(x_prompt, x_sample, norm_pre_mix, w_in, q_norm_a, k_norm_a, sink_b, w_branch_a, w_branch_b,
           w_out, norm_post_mix, norm_pre_ffn, w_up, conv_w, conv_b, w_down, norm_post_ffn):
    depth = w_in.shape[0]
    for l in range(depth):
        p = dict(
            g_pre_mix=norm_pre_mix[l][None, :], w_in=w_in[l].astype(BF16),
            gq=(q_norm_a[l] * (HEAD_DIM ** -0.5))[None, :], gk=k_norm_a[l][None, :],
            sink=sink_b[l], wa=w_branch_a[l].astype(BF16), wb=w_branch_b[l].astype(BF16),
            wo=w_out[l].astype(BF16), g_post_mix=norm_post_mix[l][None, :],
            g_pre_ffn=norm_pre_ffn[l][None, :], w_up=w_up[l].astype(BF16), conv_w=conv_w[l],
            conv_b=conv_b[l][None, :], w_down=w_down[l].astype(BF16),
            g_post_ffn=norm_post_ffn[l][None, :])
        x_prompt = _trunk(x_prompt, p)
        x_sample = _trunk(x_sample, p)
    return (x_prompt, x_sample)
```

```python
import functools
import math

import jax
import jax.numpy as jnp
from jax import lax
from jax.experimental import pallas as pl
from jax.experimental.pallas import tpu as pltpu

HEAD_DIM = 128
WINDOW = 128
GRID_W = 64
ROPE_THETA = 10000.0
EPS = 1e-6
LOG2E = 1.4426950408889634
Q_SCALE = HEAD_DIM ** -0.5 * LOG2E
LANES = 128
BF16_SUBLANES = 16
VMEM_LIMIT = 56 * 1024 * 1024

F32 = jnp.float32
BF16 = jnp.bfloat16


def _pick(n, candidates):
    for c in candidates:
        if n % c == 0:
            return c
    raise ValueError(f"no tile in {candidates} divides {n}")


def _rms(x, gain):
    r = lax.rsqrt(jnp.mean(x * x, axis=-1, keepdims=True) + EPS)
    return x * r * gain


def _params(sem):
    return pltpu.CompilerParams(dimension_semantics=sem, vmem_limit_bytes=VMEM_LIMIT)


def _runs(js):
    runs, lo, prev = [], js[0], js[0]
    for j in js[1:]:
        if j != prev + 1:
            runs.append((lo, prev))
            lo = j
        prev = j
    runs.append((lo, prev))
    return runs


def _in_proj_kernel(groups, x_ref, g_ref, w_ref, gq_ref, gk_ref, cosa_ref, sa1_ref, sa2_ref,
                    cost_ref, sint_ref, o_ref, hs_ref):
    j = pl.program_id(1)

    @pl.when(j == 0)
    def _():
        hs_ref[...] = _rms(x_ref[...], g_ref[...]).astype(BF16)

    acc = jnp.dot(hs_ref[...], w_ref[...], preferred_element_type=F32)

    def axial(h):
        return (h * cosa_ref[...] + pltpu.roll(h, 96, 1) * sa1_ref[...]
                + pltpu.roll(h, 32, 1) * sa2_ref[...])

    def rope1d(h):
        return h * cost_ref[...] + pltpu.roll(h, 64, 1) * sint_ref[...]

    def epilogue(kind, h):
        if kind == "qa":
            return axial(_rms(h, gq_ref[...]))
        if kind == "ka":
            return axial(_rms(h, gk_ref[...]))
        if kind == "qb":
            return rope1d(h) * Q_SCALE
        if kind == "kb":
            return rope1d(h)
        if kind == "gate":
            return jax.nn.sigmoid(h)
        return h

    for pattern, js in groups:
        cond = None
        for lo, hi in _runs(js):
            c = (j == lo) if lo == hi else ((j >= lo) & (j <= hi))
            cond = c if cond is None else (cond | c)

        @pl.when(cond)
        def _(pattern=pattern):
            for c, kind in enumerate(pattern):
                sl = slice(c * LANES, (c + 1) * LANES)
                o_ref[:, sl] = epilogue(kind, acc[:, sl]).astype(o_ref.dtype)


def _in_proj(x, gain, w, gq, gk, tabs, seq, seg_kinds):
    T, D = x.shape
    N = w.shape[1]
    tm = _pick(seq, (1024, 512, 256, 128))
    tn = _pick(N, (512, 256, 128))
    chunk_kind = []
    for kind, n in seg_kinds:
        chunk_kind += [kind] * (n // LANES)
    per = tn // LANES
    patterns = {}
    for j in range(N // tn):
        patterns.setdefault(tuple(chunk_kind[j * per:(j + 1) * per]), []).append(j)
    groups = tuple(patterns.items())
    seq_tiles = seq // tm
    tab_spec = pl.BlockSpec((tm, LANES), lambda i, j: (i % seq_tiles, 0))
    vec = lambda n: pl.BlockSpec((1, n), lambda i, j: (0, 0))
    return pl.pallas_call(
        functools.partial(_in_proj_kernel, groups),
        grid=(T // tm, N // tn),
        in_specs=[pl.BlockSpec((tm, D), lambda i, j: (i, 0)), vec(D),
                  pl.BlockSpec((D, tn), lambda i, j: (0, j)), vec(LANES), vec(LANES),
                  tab_spec, tab_spec, tab_spec, tab_spec, tab_spec],
        out_specs=pl.BlockSpec((tm, tn), lambda i, j: (i, j)),
        out_shape=jax.ShapeDtypeStruct((T, N), BF16),
        scratch_shapes=[pltpu.VMEM((tm, D), BF16)],
        compiler_params=_params(("parallel", "arbitrary")),
        name="in_proj",
    )(x, gain, w, gq, gk, *tabs)


def _stack_heads(q, group):
    return jnp.concatenate([q[:, g * HEAD_DIM:(g + 1) * HEAD_DIM] for g in range(group)], axis=0)


def _unstack_heads(o_ref, o, group, tq):
    for g in range(group):
        o_ref[:, g * HEAD_DIM:(g + 1) * HEAD_DIM] = o[g * tq:(g + 1) * tq].astype(o_ref.dtype)


def _qk(q, k):
    return lax.dot_general(q, k, (((1,), (1,)), ((), ())), preferred_element_type=F32)


ROW_BLOCK = 64


def _attn_dense_kernel(group, tq, tk, q_ref, k_ref, v_ref, o_ref,
                       s_bufs, p_bufs, alpha_bufs, m_ref, acc_ref, qs_ref):
    rows = group * tq
    n = k_ref.shape[0] // tk
    lane_tiles = tk // LANES
    for g in range(group):
        qs_ref[g * tq:(g + 1) * tq, :] = q_ref[:, g * HEAD_DIM:(g + 1) * HEAD_DIM]
    m_ref[...] = jnp.full(m_ref.shape, -jnp.inf, F32)
    acc_ref[...] = jnp.zeros(acc_ref.shape, F32)

    def chunk(ref, c):
        start = c * tk if isinstance(c, int) else pl.multiple_of(c * tk, tk)
        return ref[pl.ds(start, tk), :]

    def scores(c, buf):
        s_bufs[buf][...] = _qk(qs_ref[...], chunk(k_ref, c))

    def weighted(c, buf):
        v = chunk(v_ref, c)
        pv = jnp.dot(p_bufs[buf][...], jnp.concatenate([v, jnp.ones_like(v)], axis=1),
                     preferred_element_type=F32)
        for r in range(0, rows, ROW_BLOCK):
            sl = slice(r, r + ROW_BLOCK)
            alpha = alpha_bufs[buf][sl, :]
            acc_ref[sl, :] = jnp.concatenate([alpha, alpha], axis=1) * acc_ref[sl, :] + pv[sl]

    def softmax(buf):
        s_ref, p_ref, alpha_ref = s_bufs[buf], p_bufs[buf], alpha_bufs[buf]
        for r in range(0, rows, ROW_BLOCK):
            sl = slice(r, r + ROW_BLOCK)
            m_old = m_ref[sl, :]
            m_new = jnp.maximum(m_old, jnp.max(s_ref[sl, :], axis=-1, keepdims=True))
            alpha_ref[sl, :] = jnp.exp2(m_old - m_new)
            m_ref[sl, :] = m_new
        for r in range(0, rows, ROW_BLOCK):
            sl = slice(r, r + ROW_BLOCK)
            m_new = m_ref[sl, :]
            p_ref[sl, :] = jnp.concatenate(
                [jnp.exp2(s_ref[sl, j * LANES:(j + 1) * LANES] - m_new) for j in range(lane_tiles)],
                axis=1).astype(BF16)

    scores(0, 0)
    if n > 1:
        scores(1, 1)
    softmax(0)

    def body(i, carry):
        c = 2 * i + 1
        scores(c + 1, 0)
        weighted(c - 1, 0)
        softmax(1)
        scores(c + 2, 1)
        weighted(c, 1)
        softmax(0)
        return carry

    if n > 2:
        lax.fori_loop(0, (n - 2) // 2, body, 0)
    if n > 1:
        weighted(n - 2, 0)
        softmax(1)
    weighted(n - 1, (n - 1) % 2)
    acc = acc_ref[...]
    _unstack_heads(o_ref, acc[:, :HEAD_DIM] / acc[:, HEAD_DIM:], group, tq)


def _attn_dense(proj, batch, seq, n_kv, group, q_col, k_col, v_col):
    T = proj.shape[0]
    tq = _pick(seq, (256, 128))
    tk = _pick(seq, (512, 256, 128))
    assert seq // tk == 1 or (seq // tk) % 2 == 0
    nq = seq // tq
    qw = group * HEAD_DIM
    rows = group * tq
    pair = lambda shape, dt: [pltpu.VMEM(shape, dt), pltpu.VMEM(shape, dt)]
    return pl.pallas_call(
        functools.partial(_attn_dense_kernel, group, tq, tk),
        grid=(batch, n_kv, nq),
        in_specs=[pl.BlockSpec((tq, qw), lambda b, h, i: (b * nq + i, q_col // qw + h)),
                  pl.BlockSpec((seq, HEAD_DIM), lambda b, h, i: (b, k_col // HEAD_DIM + h)),
                  pl.BlockSpec((seq, HEAD_DIM), lambda b, h, i: (b, v_col // HEAD_DIM + h))],
        out_specs=pl.BlockSpec((tq, qw), lambda b, h, i: (b * nq + i, h)),
        out_shape=jax.ShapeDtypeStruct((T, n_kv * qw), BF16),
        scratch_shapes=[pair((rows, tk), F32), pair((rows, tk), BF16), pair((rows, LANES), F32),
                        pltpu.VMEM((rows, LANES), F32), pltpu.VMEM((rows, 2 * HEAD_DIM), F32),
                        pltpu.VMEM((rows, HEAD_DIM), BF16)],
        compiler_params=_params(("parallel", "parallel", "arbitrary")),
        name="attn_dense",
    )(proj, proj, proj)


def _attn_win_kernel(group, tq, span, sink_ref, bias_ref, q_ref, k_ref, v_ref, o_ref):
    h = pl.program_id(1)
    qi = pl.program_id(2)
    seq = k_ref.shape[0]
    start = pl.multiple_of(jnp.clip(qi * tq - WINDOW, 0, seq - span), LANES)
    k = k_ref[pl.ds(start, span), :]
    v = v_ref[pl.ds(start, span), :]
    q = _stack_heads(q_ref[...], group)
    s = _qk(q, k)
    s = s + jnp.concatenate([bias_ref[...]] * group, axis=0)
    sink = jnp.concatenate([jnp.full((tq, 1), sink_ref[h * group + g] * LOG2E, F32)
                            for g in range(group)], axis=0)
    m = jnp.maximum(jnp.max(s, axis=-1, keepdims=True), sink)
    ps = [jnp.exp2(s[r:r + ROW_BLOCK] - m[r:r + ROW_BLOCK]).astype(BF16)
          for r in range(0, group * tq, ROW_BLOCK)]
    v_ext = jnp.concatenate([v, jnp.ones_like(v)], axis=1)
    pv = jnp.dot(jnp.concatenate(ps, axis=0), v_ext, preferred_element_type=F32)
    denom = pv[:, HEAD_DIM:] + jnp.exp2(sink - m)
    _unstack_heads(o_ref, pv[:, :HEAD_DIM] / denom, group, tq)


def _band_bias(tq, span, seq):
    r = jnp.arange(tq)[:, None]
    c = jnp.arange(span)[None, :]
    shifts = (0, -WINDOW, tq - span)
    return jnp.stack([jnp.where(jnp.abs(c + d - r) <= WINDOW, 0.0, -jnp.inf).astype(F32) for d in shifts])


def _attn_win(proj, sink, batch, seq, n_kv, group, q_col, k_col, v_col):
    T = proj.shape[0]
    tq = _pick(seq, (256, 128))
    span = min(tq + 2 * WINDOW, seq)
    nq = seq // tq
    qw = group * HEAD_DIM
    kind = lambda i: jnp.where(i == 0, 0, jnp.where(i == nq - 1, 2, 1))
    return pl.pallas_call(
        functools.partial(_attn_win_kernel, group, tq, span),
        grid=(batch, n_kv, nq),
        in_specs=[pl.BlockSpec(memory_space=pltpu.SMEM),
                  pl.BlockSpec((None, tq, span), lambda b, h, i: (kind(i), 0, 0)),
                  pl.BlockSpec((tq, qw), lambda b, h, i: (b * nq + i, q_col // qw + h)),
                  pl.BlockSpec((seq, HEAD_DIM), lambda b, h, i: (b, k_col // HEAD_DIM + h)),
                  pl.BlockSpec((seq, HEAD_DIM), lambda b, h, i: (b, v_col // HEAD_DIM + h))],
        out_specs=pl.BlockSpec((tq, qw), lambda b, h, i: (b * nq + i, h)),
        out_shape=jax.ShapeDtypeStruct((T, n_kv * qw), BF16),
        compiler_params=_params(("parallel", "parallel", "arbitrary")),
        name="attn_win",
    )(sink, _band_bias(tq, span, seq), proj, proj, proj)


def _merge_kernel(n_gc, gw, *refs):
    oa_ref, ob_ref = refs[0], refs[1]
    ga_refs = refs[2:2 + n_gc]
    gb_refs = refs[2 + n_gc:2 + 2 * n_gc]
    x_ref, wa_ref, wb_ref, wo_ref, gpost_ref, gffn_ref, x1_ref, h2_ref = refs[2 + 2 * n_gc:]
    oa = oa_ref[...]
    ob = ob_ref[...]
    parts = []
    for c in range(n_gc):
        sl = slice(c * gw, (c + 1) * gw)
        a = jnp.dot(oa, wa_ref[:, sl], preferred_element_type=F32)
        b = jnp.dot(ob, wb_ref[:, sl], preferred_element_type=F32)
        merged = ga_refs[c][...].astype(F32) * a + gb_refs[c][...].astype(F32) * b
        parts.append(merged.astype(BF16))
    merged = jnp.concatenate(parts, axis=1)
    mix = jnp.dot(merged, wo_ref[...], preferred_element_type=F32)
    x1 = x_ref[...] + _rms(mix, gpost_ref[...])
    x1_ref[...] = x1
    h2_ref[...] = _rms(x1, gffn_ref[...]).astype(BF16)


def _resident(shape):
    return pl.BlockSpec(shape, lambda i: (0, 0), pipeline_mode=pl.Buffered(1))


def _merge_out(oa, ob, proj, x, wa, wb, wo, gpost, gffn, gate_col):
    T, D = x.shape
    qa_w, qb_w = oa.shape[1], ob.shape[1]
    tm = _pick(T, (256, 128))
    gw = math.gcd(gate_col, D)
    n_gc = D // gw
    gate_specs = [pl.BlockSpec((tm, gw), functools.partial(lambda i, c: (i, c), c=gate_col // gw + c))
                  for c in range(2 * n_gc)]
    vec = pl.BlockSpec((1, D), lambda i: (0, 0))
    row = lambda n: pl.BlockSpec((tm, n), lambda i: (i, 0))
    return pl.pallas_call(
        functools.partial(_merge_kernel, n_gc, gw),
        grid=(T // tm,),
        in_specs=[row(qa_w), row(qb_w), *gate_specs, row(D),
                  _resident(wa.shape), _resident(wb.shape), _resident(wo.shape), vec, vec],
        out_specs=[row(D), row(D)],
        out_shape=[jax.ShapeDtypeStruct((T, D), F32), jax.ShapeDtypeStruct((T, D), BF16)],
        compiler_params=_params(("parallel",)),
        name="merge_out",
    )(oa, ob, *([proj] * (2 * n_gc)), x, wa, wb, wo, gpost, gffn)


def _ffn_up_kernel(tm, seq_tiles, hm_ref, hp_ref, hn_ref, wa_ref, wb_ref, cwa_ref, cwb_ref,
                   cba_ref, cbb_ref, o_ref, hs_ref):
    i = pl.program_id(0)
    j = pl.program_id(1)
    halo = BF16_SUBLANES

    @pl.when(j == 0)
    def _():
        first = (i % seq_tiles) == 0
        last = (i % seq_tiles) == seq_tiles - 1
        hp = hp_ref[...]
        hn = hn_ref[...]
        hs_ref[0:halo, :] = jnp.where(first, jnp.zeros_like(hp), hp)
        hs_ref[halo:halo + tm, :] = hm_ref[...]
        hs_ref[halo + tm:, :] = jnp.where(last, jnp.zeros_like(hn), hn)

    hs = hs_ref[...]

    def conv_half(w_ref, cw_ref, cb_ref):
        u = jnp.dot(hs, w_ref[...], preferred_element_type=F32)
        cw = cw_ref[...]
        c = cb_ref[...] + u[halo - 1:halo - 1 + tm] * cw[0:1]
        c = c + u[halo:halo + tm] * cw[1:2]
        return c + u[halo + 1:halo + 1 + tm] * cw[2:3]

    a = conv_half(wa_ref, cwa_ref, cba_ref)
    b = conv_half(wb_ref, cwb_ref, cbb_ref)
    o_ref[...] = (jax.nn.gelu(a, approximate=True) * b).astype(o_ref.dtype)


def _ffn_up(h2, w_up, conv_w, conv_b, seq):
    T, D = h2.shape
    F = w_up.shape[1] // 2
    tm = _pick(seq, (1024, 512, 256, 128))
    tn = _pick(F, (512, 256, 128))
    nb = F // tn
    halo = BF16_SUBLANES
    per = tm // halo
    n_halo_blocks = T // halo
    seq_tiles = seq // tm
    return pl.pallas_call(
        functools.partial(_ffn_up_kernel, tm, seq_tiles),
        grid=(T // tm, nb),
        in_specs=[pl.BlockSpec((tm, D), lambda i, j: (i, 0)),
                  pl.BlockSpec((halo, D), lambda i, j: (jnp.maximum(i * per - 1, 0), 0)),
                  pl.BlockSpec((halo, D), lambda i, j: (jnp.minimum((i + 1) * per, n_halo_blocks - 1), 0)),
                  pl.BlockSpec((D, tn), lambda i, j: (0, j)),
                  pl.BlockSpec((D, tn), lambda i, j: (0, nb + j)),
                  pl.BlockSpec((3, tn), lambda i, j: (0, j)),
                  pl.BlockSpec((3, tn), lambda i, j: (0, nb + j)),
                  pl.BlockSpec((1, tn), lambda i, j: (0, j)),
                  pl.BlockSpec((1, tn), lambda i, j: (0, nb + j))],
        out_specs=pl.BlockSpec((tm, tn), lambda i, j: (i, j)),
        out_shape=jax.ShapeDtypeStruct((T, F), BF16),
        scratch_shapes=[pltpu.VMEM((tm + 2 * halo, D), BF16)],
        compiler_params=_params(("parallel", "arbitrary")),
        name="ffn_up",
    )(h2, h2, h2, w_up, w_up, conv_w, conv_w, conv_b, conv_b)


def _ffn_down_kernel(g_ref, w_ref, x1_ref, gain_ref, o_ref):
    f = jnp.dot(g_ref[...], w_ref[...], preferred_element_type=F32)
    o_ref[...] = x1_ref[...] + _rms(f, gain_ref[...])


def _ffn_down(g, w_down, x1, gain):
    T, F = g.shape
    D = x1.shape[1]
    tm = _pick(T, (256, 128))
    return pl.pallas_call(
        _ffn_down_kernel,
        grid=(T // tm,),
        in_specs=[pl.BlockSpec((tm, F), lambda i: (i, 0)), _resident(w_down.shape),
                  pl.BlockSpec((tm, D), lambda i: (i, 0)), pl.BlockSpec((1, D), lambda i: (0, 0))],
        out_specs=pl.BlockSpec((tm, D), lambda i: (i, 0)),
        out_shape=jax.ShapeDtypeStruct((T, D), F32),
        compiler_params=_params(("parallel",)),
        name="ffn_down",
    )(g, w_down, x1, gain)


def _rope_angles(pos, dim):
    inv_freq = ROPE_THETA ** (-jnp.arange(0, dim, 2, dtype=F32) / dim)
    ang = pos.astype(F32)[:, None] * inv_freq[None, :]
    return jnp.concatenate([ang, ang], axis=-1)


def _rope_tables(seq):
    t = jnp.arange(seq, dtype=jnp.int32)
    half = HEAD_DIM // 2
    ang_a = jnp.concatenate([_rope_angles(t // GRID_W, half), _rope_angles(t % GRID_W, half)], axis=-1)
    lane = jnp.arange(HEAD_DIM)[None, :]
    low = (lane % half) < (half // 2)
    cos_a, sin_a = jnp.cos(ang_a), jnp.sin(ang_a)
    sa1 = jnp.where(low, -sin_a, 0.0)
    sa2 = jnp.where(low, 0.0, sin_a)
    ang_t = _rope_angles(t, HEAD_DIM)
    sin_t = jnp.where(lane < half, -jnp.sin(ang_t), jnp.sin(ang_t))
    return cos_a, sa1, sa2, jnp.cos(ang_t), sin_t


def _trunk(x, p):
    B, S, D = x.shape
    T = B * S
    qa_w = p["wa"].shape[0]
    qb_w = p["wb"].shape[0]
    n_cols = p["w_in"].shape[1]
    kv_w = (n_cols - 2 * D - qa_w - qb_w) // 4
    n_kv = kv_w // HEAD_DIM
    group_a = qa_w // kv_w
    group_b = qb_w // kv_w
    seg = (("qa", qa_w), ("ka", kv_w), ("va", kv_w), ("qb", qb_w), ("kb", kv_w), ("vb", kv_w),
           ("gate", 2 * D))
    col = {}
    off = 0
    for kind, n in seg:
        col[kind] = off
        off += n

    x2 = x.reshape(T, D)
    proj = _in_proj(x2, p["g_pre_mix"], p["w_in"], p["gq"], p["gk"], _rope_tables(S), S, seg)
    oa = _attn_dense(proj, B, S, n_kv, group_a, col["qa"], col["ka"], col["va"])
    ob = _attn_win(proj, p["sink"], B, S, n_kv, group_b, col["qb"], col["kb"], col["vb"])
    x1, h2 = _merge_out(oa, ob, proj, x2, p["wa"], p["wb"], p["wo"], p["g_post_mix"], p["g_pre_ffn"],
                        col["gate"])
    g = _ffn_up(h2, p["w_up"], p["conv_w"], p["conv_b"], S)
    y = _ffn_down(g, p["w_down"], x1, p["g_post_ffn"])
    return y.reshape(B, S, D)


def kernel(x_prompt, x_sample, norm_pre_mix, w_in, q_norm_a, k_norm_a, sink_b, w_branch_a, w_branch_b,
           w_out, norm_post_mix, norm_pre_ffn, w_up, conv_w, conv_b, w_down, norm_post_ffn):
    depth = w_in.shape[0]
    for l in range(depth):
        p = dict(
            g_pre_mix=norm_pre_mix[l][None, :], w_in=w_in[l].astype(BF16),
            gq=(q_norm_a[l] * Q_SCALE)[None, :], gk=k_norm_a[l][None, :],
            sink=sink_b[l], wa=w_branch_a[l].astype(BF16), wb=w_branch_b[l].astype(BF16),
            wo=w_out[l].astype(BF16), g_post_mix=norm_post_mix[l][None, :],
            g_pre_ffn=norm_pre_ffn[l][None, :], w_up=w_up[l].astype(BF16), conv_w=conv_w[l],
            conv_b=conv_b[l][None, :], w_down=w_down[l].astype(BF16),
            g_post_ffn=norm_post_ffn[l][None, :])
        x_prompt = _trunk(x_prompt, p)
        x_sample = _trunk(x_sample, p)
    return (x_prompt, x_sample)
```

```python
import functools
import math

import jax
import jax.numpy as jnp
from jax import lax
from jax.experimental import pallas as pl
from jax.experimental.pallas import tpu as pltpu

HEAD_DIM = 128
WINDOW = 128
GRID_W = 64
ROPE_THETA = 10000.0
EPS = 1e-6
LOG2E = 1.4426950408889634
Q_SCALE = HEAD_DIM ** -0.5 * LOG2E
LANES = 128
BF16_SUBLANES = 16
VMEM_LIMIT = 56 * 1024 * 1024

F32 = jnp.float32
BF16 = jnp.bfloat16


def _pick(n, candidates):
    for c in candidates:
        if n % c == 0:
            return c
    raise ValueError(f"no tile in {candidates} divides {n}")


def _rms(x, gain):
    r = lax.rsqrt(jnp.mean(x * x, axis=-1, keepdims=True) + EPS)
    return x * r * gain


def _gelu_tanh(x):
    k = math.sqrt(2.0 / math.pi)
    half = 0.5 * x
    return half + half * jnp.tanh(x * (k + (k * 0.044715) * (x * x)))


def _params(sem):
    return pltpu.CompilerParams(dimension_semantics=sem, vmem_limit_bytes=VMEM_LIMIT)


def _pair(shape, dtype):
    return [pltpu.VMEM(shape, dtype), pltpu.VMEM(shape, dtype)]


def _runs(js):
    runs, lo, prev = [], js[0], js[0]
    for j in js[1:]:
        if j != prev + 1:
            runs.append((lo, prev))
            lo = j
        prev = j
    runs.append((lo, prev))
    return runs


def _in_proj_kernel(groups, x_ref, g_ref, w_ref, gq_ref, gk_ref, cosa_ref, sa1_ref, sa2_ref,
                    cost_ref, sint_ref, o_ref, hs_ref):
    j = pl.program_id(1)

    @pl.when(j == 0)
    def _():
        hs_ref[...] = _rms(x_ref[...], g_ref[...]).astype(BF16)

    tn = o_ref.shape[1]
    sub = min(tn, 4 * LANES)

    def axial(h):
        return (h * cosa_ref[...] + pltpu.roll(h, 96, 1) * sa1_ref[...]
                + pltpu.roll(h, 32, 1) * sa2_ref[...])

    def rope1d(h):
        return h * cost_ref[...] + pltpu.roll(h, 64, 1) * sint_ref[...]

    def epilogue(kind, h):
        if kind == "qa":
            return axial(_rms(h, gq_ref[...]))
        if kind == "ka":
            return axial(_rms(h, gk_ref[...]))
        if kind == "qb":
            return rope1d(h) * Q_SCALE
        if kind == "kb":
            return rope1d(h)
        if kind == "gate":
            return 0.5 + 0.5 * jnp.tanh(0.5 * h)
        return h

    for pattern, js in groups:
        cond = None
        for lo, hi in _runs(js):
            c = (j == lo) if lo == hi else ((j >= lo) & (j <= hi))
            cond = c if cond is None else (cond | c)

        @pl.when(cond)
        def _(pattern=pattern):
            for s0 in range(0, tn, sub):
                acc = jnp.dot(hs_ref[...], w_ref[:, s0:s0 + sub], preferred_element_type=F32)
                for c in range(sub // LANES):
                    sl = slice(s0 + c * LANES, s0 + (c + 1) * LANES)
                    kind = pattern[sl.start // LANES]
                    o_ref[:, sl] = epilogue(kind, acc[:, c * LANES:(c + 1) * LANES]).astype(o_ref.dtype)


def _in_proj(x, gain, w, gq, gk, tabs, seq, seg_kinds):
    T, D = x.shape
    N = w.shape[1]
    tm = _pick(seq, (1024, 512, 256, 128))
    tn = _pick(N, (1024, 512, 256, 128))
    chunk_kind = []
    for kind, n in seg_kinds:
        chunk_kind += [kind] * (n // LANES)
    per = tn // LANES
    patterns = {}
    for j in range(N // tn):
        patterns.setdefault(tuple(chunk_kind[j * per:(j + 1) * per]), []).append(j)
    groups = tuple(patterns.items())
    seq_tiles = seq // tm
    tab_spec = pl.BlockSpec((tm, LANES), lambda i, j: (i % seq_tiles, 0))
    vec = lambda n: pl.BlockSpec((1, n), lambda i, j: (0, 0))
    return pl.pallas_call(
        functools.partial(_in_proj_kernel, groups),
        grid=(T // tm, N // tn),
        in_specs=[pl.BlockSpec((tm, D), lambda i, j: (i, 0)), vec(D),
                  pl.BlockSpec((D, tn), lambda i, j: (0, j)), vec(LANES), vec(LANES),
                  tab_spec, tab_spec, tab_spec, tab_spec, tab_spec],
        out_specs=pl.BlockSpec((tm, tn), lambda i, j: (i, j)),
        out_shape=jax.ShapeDtypeStruct((T, N), BF16),
        scratch_shapes=[pltpu.VMEM((tm, D), BF16)],
        compiler_params=_params(("parallel", "arbitrary")),
        name="in_proj",
    )(x, gain, w, gq, gk, *tabs)


def _unstack_heads(o_ref, o, group, tq):
    for g in range(group):
        o_ref[:, g * HEAD_DIM:(g + 1) * HEAD_DIM] = o[g * tq:(g + 1) * tq].astype(o_ref.dtype)


def _qk(q, k):
    return lax.dot_general(q, k, (((1,), (1,)), ((), ())), preferred_element_type=F32)


ROW_BLOCK = 64


def _attn_kernel(group, tq, tk, span, sink_ref, bias_ref, qa_ref, ka_ref, va_ref, qb_ref, kb_ref, vb_ref,
                 oa_ref, ob_ref, s_bufs, p_bufs, alpha_bufs, m_ref, acc_ref, qs_ref, qw_ref, wden_ref):
    h = pl.program_id(1)
    qi = pl.program_id(2)
    seq = ka_ref.shape[0]
    rows = group * tq
    n = seq // tk
    for g in range(group):
        qs_ref[g * tq:(g + 1) * tq, :] = qa_ref[:, g * HEAD_DIM:(g + 1) * HEAD_DIM]
        qw_ref[g * tq:(g + 1) * tq, :] = qb_ref[:, g * HEAD_DIM:(g + 1) * HEAD_DIM]
    m_ref[...] = jnp.full(m_ref.shape, -jnp.inf, F32)
    acc_ref[...] = jnp.zeros(acc_ref.shape, F32)
    row_blocks = [slice(r, r + ROW_BLOCK) for r in range(0, rows, ROW_BLOCK)]

    def with_ones(v):
        return jnp.concatenate([v, jnp.ones_like(v)], axis=1)

    def exp_block(s_ref, p_ref, sl, m, width):
        p_ref[sl, :width] = jnp.concatenate(
            [jnp.exp2(s_ref[sl, j * LANES:(j + 1) * LANES] - m) for j in range(width // LANES)],
            axis=1).astype(BF16)

    start = pl.multiple_of(jnp.clip(qi * tq - WINDOW, 0, seq - span), LANES)

    def win_scores(buf):
        s_bufs[buf][:, :span] = _qk(qw_ref[...], kb_ref[pl.ds(start, span), :])

    def win_softmax(buf):
        s_ref, p_ref, m_w = s_bufs[buf], p_bufs[buf], alpha_bufs[buf]
        for sl in row_blocks:
            t = sl.start % tq
            sink = sink_ref[h * group + sl.start // tq] * LOG2E
            sb = s_ref[sl, :span] + bias_ref[t:t + ROW_BLOCK, :]
            s_ref[sl, :span] = sb
            m = jnp.maximum(jnp.max(sb, axis=-1, keepdims=True), sink)
            m_w[sl, :] = jnp.broadcast_to(m, (ROW_BLOCK, LANES))
            wden_ref[sl, :] = jnp.broadcast_to(jnp.exp2(sink - m), (ROW_BLOCK, LANES))
        for sl in row_blocks:
            exp_block(s_ref, p_ref, sl, m_w[sl, :], span)

    def win_weighted(buf):
        pv = jnp.dot(p_bufs[buf][:, :span], with_ones(vb_ref[pl.ds(start, span), :]),
                     preferred_element_type=F32)
        _unstack_heads(ob_ref, pv[:, :HEAD_DIM] / (pv[:, HEAD_DIM:] + wden_ref[...]), group, tq)

    def chunk(ref, c):
        first = c * tk if isinstance(c, int) else pl.multiple_of(c * tk, tk)
        return ref[pl.ds(first, tk), :]

    def scores(c, buf):
        s_bufs[buf][:, :tk] = _qk(qs_ref[...], chunk(ka_ref, c))

    def softmax(buf):
        s_ref, p_ref, alpha_ref = s_bufs[buf], p_bufs[buf], alpha_bufs[buf]
        for sl in row_blocks:
            m_old = m_ref[sl, :]
            m_new = jnp.maximum(m_old, jnp.max(s_ref[sl, :tk], axis=-1, keepdims=True))
            alpha_ref[sl, :] = jnp.exp2(m_old - m_new)
            m_ref[sl, :] = m_new
        for sl in row_blocks:
            exp_block(s_ref, p_ref, sl, m_ref[sl, :], tk)

    def weighted(c, buf):
        pv = jnp.dot(p_bufs[buf][:, :tk], with_ones(chunk(va_ref, c)), preferred_element_type=F32)
        for sl in row_blocks:
            alpha = alpha_bufs[buf][sl, :]
            acc_ref[sl, :] = jnp.concatenate([alpha, alpha], axis=1) * acc_ref[sl, :] + pv[sl]

    win_scores(0)
    scores(0, 1)
    win_softmax(0)

    scores(1, 0)
    win_weighted(0)
    softmax(1)

    def body(i, carry):
        c = 2 * i + 2
        scores(c, 1)
        weighted(c - 2, 1)
        softmax(0)
        scores(c + 1, 0)
        weighted(c - 1, 0)
        softmax(1)
        return carry

    if n > 2:
        lax.fori_loop(0, (n - 2) // 2, body, 0)
    weighted(n - 2, 1)
    softmax(0)
    weighted(n - 1, 0)
    acc = acc_ref[...]
    _unstack_heads(oa_ref, acc[:, :HEAD_DIM] / acc[:, HEAD_DIM:], group, tq)


def _band_bias(tq, span):
    r = jnp.arange(tq)[:, None]
    c = jnp.arange(span)[None, :]
    shifts = (0, -WINDOW, tq - span)
    return jnp.stack([jnp.where(jnp.abs(c + d - r) <= WINDOW, 0.0, -jnp.inf).astype(F32) for d in shifts])


def _attention(proj, sink, batch, seq, n_kv, group, col):
    T = proj.shape[0]
    tq = _pick(seq, (256, 128))
    tk = _pick(seq, (512, 256, 128))
    span = min(tq + 2 * WINDOW, seq)
    assert (seq // tk) % 2 == 0
    width = max(tk, span)
    nq = seq // tq
    qw = group * HEAD_DIM
    rows = group * tq
    kind = lambda i: jnp.where(i == 0, 0, jnp.where(i == nq - 1, 2, 1))
    q_spec = lambda c: pl.BlockSpec((tq, qw), lambda b, h, i: (b * nq + i, c // qw + h))
    kv_spec = lambda c: pl.BlockSpec((seq, HEAD_DIM), lambda b, h, i: (b, c // HEAD_DIM + h))
    o_spec = pl.BlockSpec((tq, qw), lambda b, h, i: (b * nq + i, h))
    o_shape = jax.ShapeDtypeStruct((T, n_kv * qw), BF16)
    return pl.pallas_call(
        functools.partial(_attn_kernel, group, tq, tk, span),
        grid=(batch, n_kv, nq),
        in_specs=[pl.BlockSpec(memory_space=pltpu.SMEM),
                  pl.BlockSpec((None, tq, span), lambda b, h, i: (kind(i), 0, 0)),
                  q_spec(col["qa"]), kv_spec(col["ka"]), kv_spec(col["va"]),
                  q_spec(col["qb"]), kv_spec(col["kb"]), kv_spec(col["vb"])],
        out_specs=[o_spec, o_spec],
        out_shape=[o_shape, o_shape],
        scratch_shapes=[_pair((rows, width), F32), _pair((rows, width), BF16), _pair((rows, LANES), F32),
                        pltpu.VMEM((rows, LANES), F32), pltpu.VMEM((rows, 2 * HEAD_DIM), F32),
                        pltpu.VMEM((rows, HEAD_DIM), BF16), pltpu.VMEM((rows, HEAD_DIM), BF16),
                        pltpu.VMEM((rows, LANES), F32)],
        compiler_params=_params(("parallel", "parallel", "arbitrary")),
        name="attention",
    )(sink, _band_bias(tq, span), proj, proj, proj, proj, proj, proj)


def _merge_kernel(n_gc, gw, *refs):
    oa_ref, ob_ref = refs[0], refs[1]
    ga_refs = refs[2:2 + n_gc]
    gb_refs = refs[2 + n_gc:2 + 2 * n_gc]
    x_ref, wa_ref, wb_ref, wo_ref, gpost_ref, gffn_ref, x1_ref, h2_ref = refs[2 + 2 * n_gc:]
    oa = oa_ref[...]
    ob = ob_ref[...]
    parts = []
    for c in range(n_gc):
        sl = slice(c * gw, (c + 1) * gw)
        a = jnp.dot(oa, wa_ref[:, sl], preferred_element_type=F32)
        b = jnp.dot(ob, wb_ref[:, sl], preferred_element_type=F32)
        merged = ga_refs[c][...].astype(F32) * a + gb_refs[c][...].astype(F32) * b
        parts.append(merged.astype(BF16))
    merged = jnp.concatenate(parts, axis=1)
    mix = jnp.dot(merged, wo_ref[...], preferred_element_type=F32)
    x1 = x_ref[...] + _rms(mix, gpost_ref[...])
    x1_ref[...] = x1
    h2_ref[...] = _rms(x1, gffn_ref[...]).astype(BF16)


def _resident(shape):
    return pl.BlockSpec(shape, lambda i: (0, 0), pipeline_mode=pl.Buffered(1))


def _merge_out(oa, ob, proj, x, wa, wb, wo, gpost, gffn, gate_col):
    T, D = x.shape
    qa_w, qb_w = oa.shape[1], ob.shape[1]
    tm = _pick(T, (256, 128))
    gw = math.gcd(gate_col, D)
    n_gc = D // gw
    gate_specs = [pl.BlockSpec((tm, gw), functools.partial(lambda i, c: (i, c), c=gate_col // gw + c))
                  for c in range(2 * n_gc)]
    vec = pl.BlockSpec((1, D), lambda i: (0, 0))
    row = lambda n: pl.BlockSpec((tm, n), lambda i: (i, 0))
    return pl.pallas_call(
        functools.partial(_merge_kernel, n_gc, gw),
        grid=(T // tm,),
        in_specs=[row(qa_w), row(qb_w), *gate_specs, row(D),
                  _resident(wa.shape), _resident(wb.shape), _resident(wo.shape), vec, vec],
        out_specs=[row(D), row(D)],
        out_shape=[jax.ShapeDtypeStruct((T, D), F32), jax.ShapeDtypeStruct((T, D), BF16)],
        compiler_params=_params(("parallel",)),
        name="merge_out",
    )(oa, ob, *([proj] * (2 * n_gc)), x, wa, wb, wo, gpost, gffn)


EPI_ROWS = 32


def _ffn_up_kernel(tm, nb, seq_tiles, n_steps, hm_ref, hp_ref, hn_ref, wa_ref, wb_ref, cwa_ref, cwb_ref,
                   cba_ref, cbb_ref, o_ref, hs_ref, ua_bufs, ub_bufs):
    t = pl.program_id(0)
    halo = BF16_SUBLANES
    tn = o_ref.shape[1]
    epi_cols = min(tn, 2 * LANES)

    @pl.when((t % nb == 0) & (t < n_steps - 1))
    def _():
        i = t // nb
        first = (i % seq_tiles) == 0
        last = (i % seq_tiles) == seq_tiles - 1
        hp = hp_ref[...]
        hn = hn_ref[...]
        hs_ref[0:halo, :] = jnp.where(first, jnp.zeros_like(hp), hp)
        hs_ref[halo:halo + tm, :] = hm_ref[...]
        hs_ref[halo + tm:, :] = jnp.where(last, jnp.zeros_like(hn), hn)

    def matmuls(buf):
        ua_bufs[buf][...] = jnp.dot(hs_ref[...], wa_ref[...], preferred_element_type=F32)
        ub_bufs[buf][...] = jnp.dot(hs_ref[...], wb_ref[...], preferred_element_type=F32)

    def epilogue(buf):
        for r in range(0, tm, EPI_ROWS):
            for c0 in range(0, tn, epi_cols):
                cols = slice(c0, c0 + epi_cols)

                def conv(u_ref, cw_ref, cb_ref):
                    cw = cw_ref[:, cols]
                    c = cb_ref[:, cols] + u_ref[halo - 1 + r:halo - 1 + r + EPI_ROWS, cols] * cw[0:1]
                    c = c + u_ref[halo + r:halo + r + EPI_ROWS, cols] * cw[1:2]
                    return c + u_ref[halo + 1 + r:halo + 1 + r + EPI_ROWS, cols] * cw[2:3]

                a = conv(ua_bufs[buf], cwa_ref, cba_ref)
                b = conv(ub_bufs[buf], cwb_ref, cbb_ref)
                o_ref[r:r + EPI_ROWS, cols] = (_gelu_tanh(a) * b).astype(o_ref.dtype)

    @pl.when(t == 0)
    def _():
        matmuls(0)

    for parity in (0, 1):
        @pl.when((t > 0) & (t < n_steps - 1) & (t % 2 == parity))
        def _(parity=parity):
            epilogue(1 - parity)
            matmuls(parity)

    @pl.when(t == n_steps - 1)
    def _():
        epilogue((n_steps - 2) % 2)


def _ffn_up(h2, w_up, conv_w, conv_b, seq):
    T, D = h2.shape
    F = w_up.shape[1] // 2
    tm = _pick(seq, (1024, 512, 256, 128))
    tn = _pick(F, (512, 256, 128))
    nb = F // tn
    halo = BF16_SUBLANES
    per = tm // halo
    n_halo_blocks = T // halo
    seq_tiles = seq // tm
    n_tiles = (T // tm) * nb
    n_steps = n_tiles + 1
    ti = lambda t: jnp.minimum(t, n_tiles - 1) // nb
    tj = lambda t: jnp.minimum(t, n_tiles - 1) % nb
    ei = lambda t: jnp.maximum(t - 1, 0) // nb
    ej = lambda t: jnp.maximum(t - 1, 0) % nb
    return pl.pallas_call(
        functools.partial(_ffn_up_kernel, tm, nb, seq_tiles, n_steps),
        grid=(n_steps,),
        in_specs=[pl.BlockSpec((tm, D), lambda t: (ti(t), 0)),
                  pl.BlockSpec((halo, D), lambda t: (jnp.maximum(ti(t) * per - 1, 0), 0)),
                  pl.BlockSpec((halo, D), lambda t: (jnp.minimum((ti(t) + 1) * per, n_halo_blocks - 1), 0)),
                  pl.BlockSpec((D, tn), lambda t: (0, tj(t))),
                  pl.BlockSpec((D, tn), lambda t: (0, nb + tj(t))),
                  pl.BlockSpec((3, tn), lambda t: (0, ej(t))),
                  pl.BlockSpec((3, tn), lambda t: (0, nb + ej(t))),
                  pl.BlockSpec((1, tn), lambda t: (0, ej(t))),
                  pl.BlockSpec((1, tn), lambda t: (0, nb + ej(t)))],
        out_specs=pl.BlockSpec((tm, tn), lambda t: (ei(t), ej(t))),
        out_shape=jax.ShapeDtypeStruct((T, F), BF16),
        scratch_shapes=[pltpu.VMEM((tm + 2 * halo, D), BF16),
                        _pair((tm + 2 * halo, tn), F32), _pair((tm + 2 * halo, tn), F32)],
        compiler_params=_params(("arbitrary",)),
        name="ffn_up",
    )(h2, h2, h2, w_up, w_up, conv_w, conv_w, conv_b, conv_b)


def _ffn_down_kernel(g_ref, w_ref, x1_ref, gain_ref, o_ref):
    f = jnp.dot(g_ref[...], w_ref[...], preferred_element_type=F32)
    o_ref[...] = x1_ref[...] + _rms(f, gain_ref[...])


def _ffn_down(g, w_down, x1, gain):
    T, F = g.shape
    D = x1.shape[1]
    tm = _pick(T, (256, 128))
    return pl.pallas_call(
        _ffn_down_kernel,
        grid=(T // tm,),
        in_specs=[pl.BlockSpec((tm, F), lambda i: (i, 0)), _resident(w_down.shape),
                  pl.BlockSpec((tm, D), lambda i: (i, 0)), pl.BlockSpec((1, D), lambda i: (0, 0))],
        out_specs=pl.BlockSpec((tm, D), lambda i: (i, 0)),
        out_shape=jax.ShapeDtypeStruct((T, D), F32),
        compiler_params=_params(("parallel",)),
        name="ffn_down",
    )(g, w_down, x1, gain)


def _rope_angles(pos, dim):
    inv_freq = ROPE_THETA ** (-jnp.arange(0, dim, 2, dtype=F32) / dim)
    ang = pos.astype(F32)[:, None] * inv_freq[None, :]
    return jnp.concatenate([ang, ang], axis=-1)


def _rope_tables(seq):
    t = jnp.arange(seq, dtype=jnp.int32)
    half = HEAD_DIM // 2
    ang_a = jnp.concatenate([_rope_angles(t // GRID_W, half), _rope_angles(t % GRID_W, half)], axis=-1)
    lane = jnp.arange(HEAD_DIM)[None, :]
    low = (lane % half) < (half // 2)
    cos_a, sin_a = jnp.cos(ang_a), jnp.sin(ang_a)
    sa1 = jnp.where(low, -sin_a, 0.0)
    sa2 = jnp.where(low, 0.0, sin_a)
    ang_t = _rope_angles(t, HEAD_DIM)
    sin_t = jnp.where(lane < half, -jnp.sin(ang_t), jnp.sin(ang_t))
    return cos_a, sa1, sa2, jnp.cos(ang_t), sin_t


def _trunk(x, p):
    B, S, D = x.shape
    T = B * S
    qa_w = p["wa"].shape[0]
    qb_w = p["wb"].shape[0]
    n_cols = p["w_in"].shape[1]
    kv_w = (n_cols - 2 * D - qa_w - qb_w) // 4
    n_kv = kv_w // HEAD_DIM
    group = qa_w // kv_w
    assert qb_w == qa_w
    seg = (("qa", qa_w), ("ka", kv_w), ("va", kv_w), ("qb", qb_w), ("kb", kv_w), ("vb", kv_w),
           ("gate", 2 * D))
    col = {}
    off = 0
    for kind, n in seg:
        col[kind] = off
        off += n

    x2 = x.reshape(T, D)
    proj = _in_proj(x2, p["g_pre_mix"], p["w_in"], p["gq"], p["gk"], _rope_tables(S), S, seg)
    oa, ob = _attention(proj, p["sink"], B, S, n_kv, group, col)
    x1, h2 = _merge_out(oa, ob, proj, x2, p["wa"], p["wb"], p["wo"], p["g_post_mix"], p["g_pre_ffn"],
                        col["gate"])
    g = _ffn_up(h2, p["w_up"], p["conv_w"], p["conv_b"], S)
    y = _ffn_down(g, p["w_down"], x1, p["g_post_ffn"])
    return y.reshape(B, S, D)


def kernel(x_prompt, x_sample, norm_pre_mix, w_in, q_norm_a, k_norm_a, sink_b, w_branch_a, w_branch_b,
           w_out, norm_post_mix, norm_pre_ffn, w_up, conv_w, conv_b, w_down, norm_post_ffn):
    depth = w_in.shape[0]
    for l in range(depth):
        p = dict(
            g_pre_mix=norm_pre_mix[l][None, :], w_in=w_in[l].astype(BF16),
            gq=(q_norm_a[l] * Q_SCALE)[None, :], gk=k_norm_a[l][None, :],
            sink=sink_b[l], wa=w_branch_a[l].astype(BF16), wb=w_branch_b[l].astype(BF16),
            wo=w_out[l].astype(BF16), g_post_mix=norm_post_mix[l][None, :],
            g_pre_ffn=norm_pre_ffn[l][None, :], w_up=w_up[l].astype(BF16), conv_w=conv_w[l],
            conv_b=conv_b[l][None, :], w_down=w_down[l].astype(BF16),
            g_post_ffn=norm_post_ffn[l][None, :])
        x_prompt = _trunk(x_prompt, p)
        x_sample = _trunk(x_sample, p)
    return (x_prompt, x_sample)
```

```python
import functools
import math

import jax
import jax.numpy as jnp
from jax import lax
from jax.experimental import pallas as pl
from jax.experimental.pallas import tpu as pltpu

HEAD_DIM = 128
WINDOW = 128
GRID_W = 64
ROPE_THETA = 10000.0
EPS = 1e-6
LOG2E = 1.4426950408889634
Q_SCALE = HEAD_DIM ** -0.5 * LOG2E
LANES = 128
BF16_SUBLANES = 16
VMEM_LIMIT = 56 * 1024 * 1024

F32 = jnp.float32
BF16 = jnp.bfloat16


def _pick(n, candidates):
    for c in candidates:
        if n % c == 0:
            return c
    raise ValueError(f"no tile in {candidates} divides {n}")


def _rms(x, gain):
    r = lax.rsqrt(jnp.mean(x * x, axis=-1, keepdims=True) + EPS)
    return x * r * gain


def _gelu_tanh(x):
    k = math.sqrt(2.0 / math.pi)
    half = 0.5 * x
    return half + half * jnp.tanh(x * (k + (k * 0.044715) * (x * x)))


def _params(sem):
    return pltpu.CompilerParams(dimension_semantics=sem, vmem_limit_bytes=VMEM_LIMIT)


def _pair(shape, dtype):
    return [pltpu.VMEM(shape, dtype), pltpu.VMEM(shape, dtype)]


def _runs(js):
    runs, lo, prev = [], js[0], js[0]
    for j in js[1:]:
        if j != prev + 1:
            runs.append((lo, prev))
            lo = j
        prev = j
    runs.append((lo, prev))
    return runs


def _in_proj_kernel(groups, x_ref, g_ref, w_ref, gq_ref, gk_ref, cosa_ref, sa1_ref, sa2_ref,
                    cost_ref, sint_ref, o_ref, hs_ref):
    j = pl.program_id(1)

    @pl.when(j == 0)
    def _():
        hs_ref[...] = _rms(x_ref[...], g_ref[...]).astype(BF16)

    tn = o_ref.shape[1]
    sub = min(tn, 4 * LANES)

    def axial(h):
        return (h * cosa_ref[...] + pltpu.roll(h, 96, 1) * sa1_ref[...]
                + pltpu.roll(h, 32, 1) * sa2_ref[...])

    def rope1d(h):
        return h * cost_ref[...] + pltpu.roll(h, 64, 1) * sint_ref[...]

    def epilogue(kind, h):
        if kind == "qa":
            return axial(_rms(h, gq_ref[...]))
        if kind == "ka":
            return axial(_rms(h, gk_ref[...]))
        if kind == "qb":
            return rope1d(h) * Q_SCALE
        if kind == "kb":
            return rope1d(h)
        if kind == "gate":
            return 0.5 + 0.5 * jnp.tanh(0.5 * h)
        return h

    for pattern, js in groups:
        cond = None
        for lo, hi in _runs(js):
            c = (j == lo) if lo == hi else ((j >= lo) & (j <= hi))
            cond = c if cond is None else (cond | c)

        @pl.when(cond)
        def _(pattern=pattern):
            for s0 in range(0, tn, sub):
                acc = jnp.dot(hs_ref[...], w_ref[:, s0:s0 + sub], preferred_element_type=F32)
                for c in range(sub // LANES):
                    sl = slice(s0 + c * LANES, s0 + (c + 1) * LANES)
                    kind = pattern[sl.start // LANES]
                    o_ref[:, sl] = epilogue(kind, acc[:, c * LANES:(c + 1) * LANES]).astype(o_ref.dtype)


def _in_proj(x, gain, w, gq, gk, tabs, seq, seg_kinds):
    T, D = x.shape
    N = w.shape[1]
    tm = _pick(seq, (1024, 512, 256, 128))
    tn = _pick(N, (1024, 512, 256, 128))
    chunk_kind = []
    for kind, n in seg_kinds:
        chunk_kind += [kind] * (n // LANES)
    per = tn // LANES
    patterns = {}
    for j in range(N // tn):
        patterns.setdefault(tuple(chunk_kind[j * per:(j + 1) * per]), []).append(j)
    groups = tuple(patterns.items())
    seq_tiles = seq // tm
    tab_spec = pl.BlockSpec((tm, LANES), lambda i, j: (i % seq_tiles, 0))
    vec = lambda n: pl.BlockSpec((1, n), lambda i, j: (0, 0))
    return pl.pallas_call(
        functools.partial(_in_proj_kernel, groups),
        grid=(T // tm, N // tn),
        in_specs=[pl.BlockSpec((tm, D), lambda i, j: (i, 0)), vec(D),
                  pl.BlockSpec((D, tn), lambda i, j: (0, j)), vec(LANES), vec(LANES),
                  tab_spec, tab_spec, tab_spec, tab_spec, tab_spec],
        out_specs=pl.BlockSpec((tm, tn), lambda i, j: (i, j)),
        out_shape=jax.ShapeDtypeStruct((T, N), BF16),
        scratch_shapes=[pltpu.VMEM((tm, D), BF16)],
        compiler_params=_params(("parallel", "arbitrary")),
        name="in_proj",
    )(x, gain, w, gq, gk, *tabs)


def _unstack_heads(o_ref, o, group, tq):
    for g in range(group):
        o_ref[:, g * HEAD_DIM:(g + 1) * HEAD_DIM] = o[g * tq:(g + 1) * tq].astype(o_ref.dtype)


def _qk(q, k):
    return lax.dot_general(q, k, (((1,), (1,)), ((), ())), preferred_element_type=F32)


ROW_BLOCK = 64


def _attn_kernel(group, tq, tk, span, sink_ref, bias_ref, qa_ref, ka_ref, va_ref, qb_ref, kb_ref, vb_ref,
                 oa_ref, ob_ref, s_bufs, p_bufs, alpha_bufs, m_ref, acc_ref, qs_ref, qw_ref, wden_ref):
    h = pl.program_id(1)
    qi = pl.program_id(2)
    seq = ka_ref.shape[0]
    rows = group * tq
    n = seq // tk
    for g in range(group):
        qs_ref[g * tq:(g + 1) * tq, :] = qa_ref[:, g * HEAD_DIM:(g + 1) * HEAD_DIM]
        qw_ref[g * tq:(g + 1) * tq, :] = qb_ref[:, g * HEAD_DIM:(g + 1) * HEAD_DIM]
    m_ref[...] = jnp.full(m_ref.shape, -jnp.inf, F32)
    acc_ref[...] = jnp.zeros(acc_ref.shape, F32)
    row_blocks = [slice(r, r + ROW_BLOCK) for r in range(0, rows, ROW_BLOCK)]

    def with_ones(v):
        return jnp.concatenate([v, jnp.ones_like(v)], axis=1)

    def exp_block(s_ref, p_ref, sl, m, width):
        p_ref[sl, :width] = jnp.concatenate(
            [jnp.exp2(s_ref[sl, j * LANES:(j + 1) * LANES] - m) for j in range(width // LANES)],
            axis=1).astype(BF16)

    start = pl.multiple_of(jnp.clip(qi * tq - WINDOW, 0, seq - span), LANES)

    def win_scores(buf):
        s_bufs[buf][:, :span] = _qk(qw_ref[...], kb_ref[pl.ds(start, span), :])

    def win_softmax(buf):
        s_ref, p_ref, m_w = s_bufs[buf], p_bufs[buf], alpha_bufs[buf]
        for sl in row_blocks:
            t = sl.start % tq
            sink = sink_ref[h * group + sl.start // tq] * LOG2E
            sb = s_ref[sl, :span] + bias_ref[t:t + ROW_BLOCK, :]
            s_ref[sl, :span] = sb
            m = jnp.maximum(jnp.max(sb, axis=-1, keepdims=True), sink)
            m_w[sl, :] = jnp.broadcast_to(m, (ROW_BLOCK, LANES))
            wden_ref[sl, :] = jnp.broadcast_to(jnp.exp2(sink - m), (ROW_BLOCK, LANES))
        for sl in row_blocks:
            exp_block(s_ref, p_ref, sl, m_w[sl, :], span)

    def win_weighted(buf):
        pv = jnp.dot(p_bufs[buf][:, :span], with_ones(vb_ref[pl.ds(start, span), :]),
                     preferred_element_type=F32)
        _unstack_heads(ob_ref, pv[:, :HEAD_DIM] / (pv[:, HEAD_DIM:] + wden_ref[...]), group, tq)

    def chunk(ref, c):
        first = c * tk if isinstance(c, int) else pl.multiple_of(c * tk, tk)
        return ref[pl.ds(first, tk), :]

    def scores(c, buf):
        s_bufs[buf][:, :tk] = _qk(qs_ref[...], chunk(ka_ref, c))

    def softmax(buf):
        s_ref, p_ref, alpha_ref = s_bufs[buf], p_bufs[buf], alpha_bufs[buf]
        for sl in row_blocks:
            m_old = m_ref[sl, :]
            m_new = jnp.maximum(m_old, jnp.max(s_ref[sl, :tk], axis=-1, keepdims=True))
            alpha_ref[sl, :] = jnp.exp2(m_old - m_new)
            m_ref[sl, :] = m_new
        for sl in row_blocks:
            exp_block(s_ref, p_ref, sl, m_ref[sl, :], tk)

    def weighted(c, buf):
        pv = jnp.dot(p_bufs[buf][:, :tk], with_ones(chunk(va_ref, c)), preferred_element_type=F32)
        for sl in row_blocks:
            alpha = alpha_bufs[buf][sl, :]
            acc_ref[sl, :] = jnp.concatenate([alpha, alpha], axis=1) * acc_ref[sl, :] + pv[sl]

    win_scores(0)
    scores(0, 1)
    win_softmax(0)

    scores(1, 0)
    win_weighted(0)
    softmax(1)

    def body(i, carry):
        c = 2 * i + 2
        scores(c, 1)
        weighted(c - 2, 1)
        softmax(0)
        scores(c + 1, 0)
        weighted(c - 1, 0)
        softmax(1)
        return carry

    if n > 2:
        lax.fori_loop(0, (n - 2) // 2, body, 0)
    weighted(n - 2, 1)
    softmax(0)
    weighted(n - 1, 0)
    acc = acc_ref[...]
    _unstack_heads(oa_ref, acc[:, :HEAD_DIM] / acc[:, HEAD_DIM:], group, tq)


def _band_bias(tq, span):
    r = jnp.arange(tq)[:, None]
    c = jnp.arange(span)[None, :]
    shifts = (0, -WINDOW, tq - span)
    return jnp.stack([jnp.where(jnp.abs(c + d - r) <= WINDOW, 0.0, -jnp.inf).astype(F32) for d in shifts])


def _attention(proj, sink, batch, seq, n_kv, group, col):
    T = proj.shape[0]
    tq = _pick(seq, (256, 128))
    tk = _pick(seq, (512, 256, 128))
    span = min(tq + 2 * WINDOW, seq)
    assert (seq // tk) % 2 == 0
    width = max(tk, span)
    nq = seq // tq
    qw = group * HEAD_DIM
    rows = group * tq
    kind = lambda i: jnp.where(i == 0, 0, jnp.where(i == nq - 1, 2, 1))
    q_spec = lambda c: pl.BlockSpec((tq, qw), lambda b, h, i: (b * nq + i, c // qw + h))
    kv_spec = lambda c: pl.BlockSpec((seq, HEAD_DIM), lambda b, h, i: (b, c // HEAD_DIM + h))
    o_spec = pl.BlockSpec((tq, qw), lambda b, h, i: (b * nq + i, h))
    o_shape = jax.ShapeDtypeStruct((T, n_kv * qw), BF16)
    return pl.pallas_call(
        functools.partial(_attn_kernel, group, tq, tk, span),
        grid=(batch, n_kv, nq),
        in_specs=[pl.BlockSpec(memory_space=pltpu.SMEM),
                  pl.BlockSpec((None, tq, span), lambda b, h, i: (kind(i), 0, 0)),
                  q_spec(col["qa"]), kv_spec(col["ka"]), kv_spec(col["va"]),
                  q_spec(col["qb"]), kv_spec(col["kb"]), kv_spec(col["vb"])],
        out_specs=[o_spec, o_spec],
        out_shape=[o_shape, o_shape],
        scratch_shapes=[_pair((rows, width), F32), _pair((rows, width), BF16), _pair((rows, LANES), F32),
                        pltpu.VMEM((rows, LANES), F32), pltpu.VMEM((rows, 2 * HEAD_DIM), F32),
                        pltpu.VMEM((rows, HEAD_DIM), BF16), pltpu.VMEM((rows, HEAD_DIM), BF16),
                        pltpu.VMEM((rows, LANES), F32)],
        compiler_params=_params(("parallel", "parallel", "arbitrary")),
        name="attention",
    )(sink, _band_bias(tq, span), proj, proj, proj, proj, proj, proj)


def _merge_kernel(n_gc, gw, *refs):
    oa_ref, ob_ref = refs[0], refs[1]
    ga_refs = refs[2:2 + n_gc]
    gb_refs = refs[2 + n_gc:2 + 2 * n_gc]
    x_ref, wa_ref, wb_ref, wo_ref, gpost_ref, gffn_ref, x1_ref, h2_ref = refs[2 + 2 * n_gc:]
    oa = oa_ref[...]
    ob = ob_ref[...]
    parts = []
    for c in range(n_gc):
        sl = slice(c * gw, (c + 1) * gw)
        a = jnp.dot(oa, wa_ref[:, sl], preferred_element_type=F32)
        b = jnp.dot(ob, wb_ref[:, sl], preferred_element_type=F32)
        merged = ga_refs[c][...].astype(F32) * a + gb_refs[c][...].astype(F32) * b
        parts.append(merged.astype(BF16))
    merged = jnp.concatenate(parts, axis=1)
    mix = jnp.dot(merged, wo_ref[...], preferred_element_type=F32)
    x1 = x_ref[...] + _rms(mix, gpost_ref[...])
    x1_ref[...] = x1
    h2_ref[...] = _rms(x1, gffn_ref[...]).astype(BF16)


def _resident(shape):
    return pl.BlockSpec(shape, lambda i: (0, 0), pipeline_mode=pl.Buffered(1))


def _merge_out(oa, ob, proj, x, wa, wb, wo, gpost, gffn, gate_col):
    T, D = x.shape
    qa_w, qb_w = oa.shape[1], ob.shape[1]
    tm = _pick(T, (256, 128))
    gw = math.gcd(gate_col, D)
    n_gc = D // gw
    gate_specs = [pl.BlockSpec((tm, gw), functools.partial(lambda i, c: (i, c), c=gate_col // gw + c))
                  for c in range(2 * n_gc)]
    vec = pl.BlockSpec((1, D), lambda i: (0, 0))
    row = lambda n: pl.BlockSpec((tm, n), lambda i: (i, 0))
    return pl.pallas_call(
        functools.partial(_merge_kernel, n_gc, gw),
        grid=(T // tm,),
        in_specs=[row(qa_w), row(qb_w), *gate_specs, row(D),
                  _resident(wa.shape), _resident(wb.shape), _resident(wo.shape), vec, vec],
        out_specs=[row(D), row(D)],
        out_shape=[jax.ShapeDtypeStruct((T, D), F32), jax.ShapeDtypeStruct((T, D), BF16)],
        compiler_params=_params(("parallel",)),
        name="merge_out",
    )(oa, ob, *([proj] * (2 * n_gc)), x, wa, wb, wo, gpost, gffn)


EPI_ROWS = 32


def _ffn_up_kernel(tm, seq_tiles, hm_ref, hp_ref, hn_ref, wa_ref, wb_ref, cwa_ref, cwb_ref,
                   cba_ref, cbb_ref, o_ref, hs_ref):
    i = pl.program_id(0)
    j = pl.program_id(1)
    halo = BF16_SUBLANES
    tn = o_ref.shape[1]
    epi_cols = min(tn, 2 * LANES)

    @pl.when(j == 0)
    def _():
        first = (i % seq_tiles) == 0
        last = (i % seq_tiles) == seq_tiles - 1
        hp = hp_ref[...]
        hn = hn_ref[...]
        hs_ref[0:halo, :] = jnp.where(first, jnp.zeros_like(hp), hp)
        hs_ref[halo:halo + tm, :] = hm_ref[...]
        hs_ref[halo + tm:, :] = jnp.where(last, jnp.zeros_like(hn), hn)

    def conv(u, cw_ref, cb_ref):
        cw = cw_ref[...]
        c = cb_ref[...] + u[halo - 1:halo - 1 + tm] * cw[0:1]
        c = c + u[halo:halo + tm] * cw[1:2]
        return c + u[halo + 1:halo + 1 + tm] * cw[2:3]

    ua = jnp.dot(hs_ref[...], wa_ref[...], preferred_element_type=F32)
    a = _gelu_tanh(conv(ua, cwa_ref, cba_ref))
    ub = jnp.dot(hs_ref[...], wb_ref[...], preferred_element_type=F32)
    o_ref[...] = (a * conv(ub, cwb_ref, cbb_ref)).astype(o_ref.dtype)


def _ffn_up(h2, w_up, conv_w, conv_b, seq):
    T, D = h2.shape
    F = w_up.shape[1] // 2
    tm = _pick(seq, (1024, 512, 256, 128))
    tn = _pick(F, (512, 256, 128))
    nb = F // tn
    halo = BF16_SUBLANES
    per = tm // halo
    n_halo_blocks = T // halo
    seq_tiles = seq // tm
    return pl.pallas_call(
        functools.partial(_ffn_up_kernel, tm, seq_tiles),
        grid=(T // tm, nb),
        in_specs=[pl.BlockSpec((tm, D), lambda i, j: (i, 0)),
                  pl.BlockSpec((halo, D), lambda i, j: (jnp.maximum(i * per - 1, 0), 0)),
                  pl.BlockSpec((halo, D), lambda i, j: (jnp.minimum((i + 1) * per, n_halo_blocks - 1), 0)),
                  pl.BlockSpec((D, tn), lambda i, j: (0, j)),
                  pl.BlockSpec((D, tn), lambda i, j: (0, nb + j)),
                  pl.BlockSpec((3, tn), lambda i, j: (0, j)),
                  pl.BlockSpec((3, tn), lambda i, j: (0, nb + j)),
                  pl.BlockSpec((1, tn), lambda i, j: (0, j)),
                  pl.BlockSpec((1, tn), lambda i, j: (0, nb + j))],
        out_specs=pl.BlockSpec((tm, tn), lambda i, j: (i, j)),
        out_shape=jax.ShapeDtypeStruct((T, F), BF16),
        scratch_shapes=[pltpu.VMEM((tm + 2 * halo, D), BF16)],
        compiler_params=_params(("parallel", "arbitrary")),
        name="ffn_up",
    )(h2, h2, h2, w_up, w_up, conv_w, conv_w, conv_b, conv_b)


def _ffn_down_kernel(g_ref, w_ref, x1_ref, gain_ref, o_ref):
    f = jnp.dot(g_ref[...], w_ref[...], preferred_element_type=F32)
    o_ref[...] = x1_ref[...] + _rms(f, gain_ref[...])


def _ffn_down(g, w_down, x1, gain):
    T, F = g.shape
    D = x1.shape[1]
    tm = _pick(T, (256, 128))
    return pl.pallas_call(
        _ffn_down_kernel,
        grid=(T // tm,),
        in_specs=[pl.BlockSpec((tm, F), lambda i: (i, 0)), _resident(w_down.shape),
                  pl.BlockSpec((tm, D), lambda i: (i, 0)), pl.BlockSpec((1, D), lambda i: (0, 0))],
        out_specs=pl.BlockSpec((tm, D), lambda i: (i, 0)),
        out_shape=jax.ShapeDtypeStruct((T, D), F32),
        compiler_params=_params(("parallel",)),
        name="ffn_down",
    )(g, w_down, x1, gain)


def _rope_angles(pos, dim):
    inv_freq = ROPE_THETA ** (-jnp.arange(0, dim, 2, dtype=F32) / dim)
    ang = pos.astype(F32)[:, None] * inv_freq[None, :]
    return jnp.concatenate([ang, ang], axis=-1)


def _rope_tables(seq):
    t = jnp.arange(seq, dtype=jnp.int32)
    half = HEAD_DIM // 2
    ang_a = jnp.concatenate([_rope_angles(t // GRID_W, half), _rope_angles(t % GRID_W, half)], axis=-1)
    lane = jnp.arange(HEAD_DIM)[None, :]
    low = (lane % half) < (half // 2)
    cos_a, sin_a = jnp.cos(ang_a), jnp.sin(ang_a)
    sa1 = jnp.where(low, -sin_a, 0.0)
    sa2 = jnp.where(low, 0.0, sin_a)
    ang_t = _rope_angles(t, HEAD_DIM)
    sin_t = jnp.where(lane < half, -jnp.sin(ang_t), jnp.sin(ang_t))
    return cos_a, sa1, sa2, jnp.cos(ang_t), sin_t


def _trunk(x, p):
    B, S, D = x.shape
    T = B * S
    qa_w = p["wa"].shape[0]
    qb_w = p["wb"].shape[0]
    n_cols = p["w_in"].shape[1]
    kv_w = (n_cols - 2 * D - qa_w - qb_w) // 4
    n_kv = kv_w // HEAD_DIM
    group = qa_w // kv_w
    assert qb_w == qa_w
    seg = (("qa", qa_w), ("ka", kv_w), ("va", kv_w), ("qb", qb_w), ("kb", kv_w), ("vb", kv_w),
           ("gate", 2 * D))
    col = {}
    off = 0
    for kind, n in seg:
        col[kind] = off
        off += n

    x2 = x.reshape(T, D)
    proj = _in_proj(x2, p["g_pre_mix"], p["w_in"], p["gq"], p["gk"], _rope_tables(S), S, seg)
    oa, ob = _attention(proj, p["sink"], B, S, n_kv, group, col)
    x1, h2 = _merge_out(oa, ob, proj, x2, p["wa"], p["wb"], p["wo"], p["g_post_mix"], p["g_pre_ffn"],
                        col["gate"])
    g = _ffn_up(h2, p["w_up"], p["conv_w"], p["conv_b"], S)
    y = _ffn_down(g, p["w_down"], x1, p["g_post_ffn"])
    return y.reshape(B, S, D)


def kernel(x_prompt, x_sample, norm_pre_mix, w_in, q_norm_a, k_norm_a, sink_b, w_branch_a, w_branch_b,
           w_out, norm_post_mix, norm_pre_ffn, w_up, conv_w, conv_b, w_down, norm_post_ffn):
    depth = w_in.shape[0]
    for l in range(depth):
        p = dict(
            g_pre_mix=norm_pre_mix[l][None, :], w_in=w_in[l].astype(BF16),
            gq=(q_norm_a[l] * Q_SCALE)[None, :], gk=k_norm_a[l][None, :],
            sink=sink_b[l], wa=w_branch_a[l].astype(BF16), wb=w_branch_b[l].astype(BF16),
            wo=w_out[l].astype(BF16), g_post_mix=norm_post_mix[l][None, :],
            g_pre_ffn=norm_pre_ffn[l][None, :], w_up=w_up[l].astype(BF16), conv_w=conv_w[l],
            conv_b=conv_b[l][None, :], w_down=w_down[l].astype(BF16),
            g_post_ffn=norm_post_ffn[l][None, :])
        x_prompt = _trunk(x_prompt, p)
        x_sample = _trunk(x_sample, p)
    return (x_prompt, x_sample)
```

```python
import functools
import math

import jax
import jax.numpy as jnp
from jax import lax
from jax.experimental import pallas as pl
from jax.experimental.pallas import tpu as pltpu

HEAD_DIM = 128
WINDOW = 128
GRID_W = 64
ROPE_THETA = 10000.0
EPS = 1e-6
LOG2E = 1.4426950408889634
Q_SCALE = HEAD_DIM ** -0.5 * LOG2E
LANES = 128
BF16_SUBLANES = 16
VMEM_LIMIT = 56 * 1024 * 1024

F32 = jnp.float32
BF16 = jnp.bfloat16


def _pick(n, candidates):
    for c in candidates:
        if n % c == 0:
            return c
    raise ValueError(f"no tile in {candidates} divides {n}")


def _rms(x, gain):
    r = lax.rsqrt(jnp.mean(x * x, axis=-1, keepdims=True) + EPS)
    return x * r * gain


def _gelu_tanh(x):
    k = math.sqrt(2.0 / math.pi)
    half = 0.5 * x
    return half + half * jnp.tanh(x * (k + (k * 0.044715) * (x * x)))


def _params(sem):
    return pltpu.CompilerParams(dimension_semantics=sem, vmem_limit_bytes=VMEM_LIMIT)


def _bufs(count, shape, dtype):
    return [pltpu.VMEM(shape, dtype) for _ in range(count)]


def _runs(js):
    runs, lo, prev = [], js[0], js[0]
    for j in js[1:]:
        if j != prev + 1:
            runs.append((lo, prev))
            lo = j
        prev = j
    runs.append((lo, prev))
    return runs


def _in_proj_kernel(groups, x_ref, g_ref, w_ref, gq_ref, gk_ref, cosa_ref, sa1_ref, sa2_ref,
                    cost_ref, sint_ref, o_ref, hs_ref):
    j = pl.program_id(1)

    @pl.when(j == 0)
    def _():
        hs_ref[...] = _rms(x_ref[...], g_ref[...]).astype(BF16)

    tn = o_ref.shape[1]
    sub = min(tn, 4 * LANES)

    def axial(h):
        return (h * cosa_ref[...] + pltpu.roll(h, 96, 1) * sa1_ref[...]
                + pltpu.roll(h, 32, 1) * sa2_ref[...])

    def rope1d(h):
        return h * cost_ref[...] + pltpu.roll(h, 64, 1) * sint_ref[...]

    def epilogue(kind, h):
        if kind == "qa":
            return axial(_rms(h, gq_ref[...]))
        if kind == "ka":
            return axial(_rms(h, gk_ref[...]))
        if kind == "qb":
            return rope1d(h) * Q_SCALE
        if kind == "kb":
            return rope1d(h)
        if kind == "gate":
            return 0.5 + 0.5 * jnp.tanh(0.5 * h)
        return h

    for pattern, js in groups:
        cond = None
        for lo, hi in _runs(js):
            c = (j == lo) if lo == hi else ((j >= lo) & (j <= hi))
            cond = c if cond is None else (cond | c)

        @pl.when(cond)
        def _(pattern=pattern):
            for s0 in range(0, tn, sub):
                acc = jnp.dot(hs_ref[...], w_ref[:, s0:s0 + sub], preferred_element_type=F32)
                for c in range(sub // LANES):
                    sl = slice(s0 + c * LANES, s0 + (c + 1) * LANES)
                    kind = pattern[sl.start // LANES]
                    o_ref[:, sl] = epilogue(kind, acc[:, c * LANES:(c + 1) * LANES]).astype(o_ref.dtype)


def _in_proj(x, gain, w, gq, gk, tabs, seq, seg_kinds):
    T, D = x.shape
    N = w.shape[1]
    tm = _pick(seq, (1024, 512, 256, 128))
    tn = _pick(N, (1024, 512, 256, 128))
    chunk_kind = []
    for kind, n in seg_kinds:
        chunk_kind += [kind] * (n // LANES)
    per = tn // LANES
    patterns = {}
    for j in range(N // tn):
        patterns.setdefault(tuple(chunk_kind[j * per:(j + 1) * per]), []).append(j)
    groups = tuple(patterns.items())
    seq_tiles = seq // tm
    tab_spec = pl.BlockSpec((tm, LANES), lambda i, j: (i % seq_tiles, 0))
    vec = lambda n: pl.BlockSpec((1, n), lambda i, j: (0, 0))
    return pl.pallas_call(
        functools.partial(_in_proj_kernel, groups),
        grid=(T // tm, N // tn),
        in_specs=[pl.BlockSpec((tm, D), lambda i, j: (i, 0)), vec(D),
                  pl.BlockSpec((D, tn), lambda i, j: (0, j)), vec(LANES), vec(LANES),
                  tab_spec, tab_spec, tab_spec, tab_spec, tab_spec],
        out_specs=pl.BlockSpec((tm, tn), lambda i, j: (i, j)),
        out_shape=jax.ShapeDtypeStruct((T, N), BF16),
        scratch_shapes=[pltpu.VMEM((tm, D), BF16)],
        compiler_params=_params(("parallel", "arbitrary")),
        name="in_proj",
    )(x, gain, w, gq, gk, *tabs)


def _unstack_heads(o_ref, o, group, tq):
    for g in range(group):
        o_ref[:, g * HEAD_DIM:(g + 1) * HEAD_DIM] = o[g * tq:(g + 1) * tq].astype(o_ref.dtype)


def _qk(q, k):
    return lax.dot_general(q, k, (((1,), (1,)), ((), ())), preferred_element_type=F32)


ROW_BLOCK = 64
DEPTH = 3


def _attn_kernel(group, tq, tk, span, sink_ref, bias_ref, qa_ref, ka_ref, va_ref, qb_ref, kb_ref, vb_ref,
                 oa_ref, ob_ref, s_bufs, p_bufs, alpha_bufs, m_ref, acc_ref, qs_ref, qw_ref, wden_ref):
    h = pl.program_id(1)
    qi = pl.program_id(2)
    seq = ka_ref.shape[0]
    rows = group * tq
    n = seq // tk
    for g in range(group):
        qs_ref[g * tq:(g + 1) * tq, :] = qa_ref[:, g * HEAD_DIM:(g + 1) * HEAD_DIM]
        qw_ref[g * tq:(g + 1) * tq, :] = qb_ref[:, g * HEAD_DIM:(g + 1) * HEAD_DIM]
    m_ref[...] = jnp.full(m_ref.shape, -jnp.inf, F32)
    acc_ref[...] = jnp.zeros(acc_ref.shape, F32)
    row_blocks = [slice(r, r + ROW_BLOCK) for r in range(0, rows, ROW_BLOCK)]

    def with_ones(v):
        return jnp.concatenate([v, jnp.ones_like(v)], axis=1)

    def exp_block(s_ref, p_ref, sl, m, width):
        p_ref[sl, :width] = jnp.concatenate(
            [jnp.exp2(s_ref[sl, j * LANES:(j + 1) * LANES] - m) for j in range(width // LANES)],
            axis=1).astype(BF16)

    start = pl.multiple_of(jnp.clip(qi * tq - WINDOW, 0, seq - span), LANES)

    def win_scores(buf):
        s_bufs[buf][:, :span] = _qk(qw_ref[...], kb_ref[pl.ds(start, span), :])

    def win_softmax(buf):
        s_ref, p_ref, m_w = s_bufs[buf], p_bufs[buf], alpha_bufs[buf]
        for sl in row_blocks:
            t = sl.start % tq
            sink = sink_ref[h * group + sl.start // tq] * LOG2E
            sb = s_ref[sl, :span] + bias_ref[t:t + ROW_BLOCK, :]
            s_ref[sl, :span] = sb
            m = jnp.maximum(jnp.max(sb, axis=-1, keepdims=True), sink)
            m_w[sl, :] = jnp.broadcast_to(m, (ROW_BLOCK, LANES))
            wden_ref[sl, :] = jnp.broadcast_to(jnp.exp2(sink - m), (ROW_BLOCK, LANES))
        for sl in row_blocks:
            exp_block(s_ref, p_ref, sl, m_w[sl, :], span)

    def win_weighted(buf):
        pv = jnp.dot(p_bufs[buf][:, :span], with_ones(vb_ref[pl.ds(start, span), :]),
                     preferred_element_type=F32)
        _unstack_heads(ob_ref, pv[:, :HEAD_DIM] / (pv[:, HEAD_DIM:] + wden_ref[...]), group, tq)

    def chunk(ref, c):
        first = c * tk if isinstance(c, int) else pl.multiple_of(c * tk, tk)
        return ref[pl.ds(first, tk), :]

    def scores(c, buf):
        s_bufs[buf][:, :tk] = _qk(qs_ref[...], chunk(ka_ref, c))

    def softmax(buf):
        s_ref, p_ref, alpha_ref = s_bufs[buf], p_bufs[buf], alpha_bufs[buf]
        for sl in row_blocks:
            m_old = m_ref[sl, :]
            m_new = jnp.maximum(m_old, jnp.max(s_ref[sl, :tk], axis=-1, keepdims=True))
            alpha_ref[sl, :] = jnp.exp2(m_old - m_new)
            m_ref[sl, :] = m_new
        for sl in row_blocks:
            exp_block(s_ref, p_ref, sl, m_ref[sl, :], tk)

    def weighted(c, buf):
        pv = jnp.dot(p_bufs[buf][:, :tk], with_ones(chunk(va_ref, c)), preferred_element_type=F32)
        for sl in row_blocks:
            alpha = alpha_bufs[buf][sl, :]
            acc_ref[sl, :] = jnp.concatenate([alpha, alpha], axis=1) * acc_ref[sl, :] + pv[sl]

    n_items = n + 1

    def stage_scores(item, buf):
        if isinstance(item, int) and item == 0:
            win_scores(buf)
        else:
            scores(item - 1, buf)

    def stage_softmax(item, buf):
        if isinstance(item, int) and item == 0:
            win_softmax(buf)
        else:
            softmax(buf)

    def stage_weighted(item, buf):
        if isinstance(item, int) and item == 0:
            win_weighted(buf)
        else:
            weighted(item - 1, buf)

    def step(t, phase):
        static = isinstance(t, int)
        if not static or 0 <= t < n_items:
            stage_softmax(t, phase)
        if not static or 0 <= t + 1 < n_items:
            stage_scores(t + 1, (phase + 1) % DEPTH)
        if not static or 0 <= t - 2 < n_items:
            stage_weighted(t - 2, (phase - 2) % DEPTH)

    for t in range(-1, n_items + 2):
        step(t, t % DEPTH)
    acc = acc_ref[...]
    _unstack_heads(oa_ref, acc[:, :HEAD_DIM] / acc[:, HEAD_DIM:], group, tq)


def _band_bias(tq, span):
    r = jnp.arange(tq)[:, None]
    c = jnp.arange(span)[None, :]
    shifts = (0, -WINDOW, tq - span)
    return jnp.stack([jnp.where(jnp.abs(c + d - r) <= WINDOW, 0.0, -jnp.inf).astype(F32) for d in shifts])


def _attention(proj, sink, batch, seq, n_kv, group, col):
    T = proj.shape[0]
    tq = _pick(seq, (256, 128))
    tk = _pick(seq, (512, 256, 128))
    span = min(tq + 2 * WINDOW, seq)
    assert (seq // tk) % 2 == 0
    width = max(tk, span)
    nq = seq // tq
    qw = group * HEAD_DIM
    rows = group * tq
    kind = lambda i: jnp.where(i == 0, 0, jnp.where(i == nq - 1, 2, 1))
    q_spec = lambda c: pl.BlockSpec((tq, qw), lambda b, h, i: (b * nq + i, c // qw + h))
    kv_spec = lambda c: pl.BlockSpec((seq, HEAD_DIM), lambda b, h, i: (b, c // HEAD_DIM + h))
    o_spec = pl.BlockSpec((tq, qw), lambda b, h, i: (b * nq + i, h))
    o_shape = jax.ShapeDtypeStruct((T, n_kv * qw), BF16)
    return pl.pallas_call(
        functools.partial(_attn_kernel, group, tq, tk, span),
        grid=(batch, n_kv, nq),
        in_specs=[pl.BlockSpec(memory_space=pltpu.SMEM),
                  pl.BlockSpec((None, tq, span), lambda b, h, i: (kind(i), 0, 0)),
                  q_spec(col["qa"]), kv_spec(col["ka"]), kv_spec(col["va"]),
                  q_spec(col["qb"]), kv_spec(col["kb"]), kv_spec(col["vb"])],
        out_specs=[o_spec, o_spec],
        out_shape=[o_shape, o_shape],
        scratch_shapes=[_bufs(DEPTH, (rows, width), F32), _bufs(DEPTH, (rows, width), BF16),
                        _bufs(DEPTH, (rows, LANES), F32),
                        pltpu.VMEM((rows, LANES), F32), pltpu.VMEM((rows, 2 * HEAD_DIM), F32),
                        pltpu.VMEM((rows, HEAD_DIM), BF16), pltpu.VMEM((rows, HEAD_DIM), BF16),
                        pltpu.VMEM((rows, LANES), F32)],
        compiler_params=_params(("parallel", "parallel", "arbitrary")),
        name="attention",
    )(sink, _band_bias(tq, span), proj, proj, proj, proj, proj, proj)


def _merge_kernel(n_gc, gw, *refs):
    oa_ref, ob_ref = refs[0], refs[1]
    ga_refs = refs[2:2 + n_gc]
    gb_refs = refs[2 + n_gc:2 + 2 * n_gc]
    x_ref, wa_ref, wb_ref, wo_ref, gpost_ref, gffn_ref, x1_ref, h2_ref = refs[2 + 2 * n_gc:]
    oa = oa_ref[...]
    ob = ob_ref[...]
    parts = []
    for c in range(n_gc):
        sl = slice(c * gw, (c + 1) * gw)
        a = jnp.dot(oa, wa_ref[:, sl], preferred_element_type=F32)
        b = jnp.dot(ob, wb_ref[:, sl], preferred_element_type=F32)
        merged = ga_refs[c][...].astype(F32) * a + gb_refs[c][...].astype(F32) * b
        parts.append(merged.astype(BF16))
    merged = jnp.concatenate(parts, axis=1)
    mix = jnp.dot(merged, wo_ref[...], preferred_element_type=F32)
    x1 = x_ref[...] + _rms(mix, gpost_ref[...])
    x1_ref[...] = x1
    h2_ref[...] = _rms(x1, gffn_ref[...]).astype(BF16)


def _resident(shape):
    return pl.BlockSpec(shape, lambda i: (0, 0), pipeline_mode=pl.Buffered(1))


def _merge_out(oa, ob, proj, x, wa, wb, wo, gpost, gffn, gate_col):
    T, D = x.shape
    qa_w, qb_w = oa.shape[1], ob.shape[1]
    tm = _pick(T, (256, 128))
    gw = math.gcd(gate_col, D)
    n_gc = D // gw
    gate_specs = [pl.BlockSpec((tm, gw), functools.partial(lambda i, c: (i, c), c=gate_col // gw + c))
                  for c in range(2 * n_gc)]
    vec = pl.BlockSpec((1, D), lambda i: (0, 0))
    row = lambda n: pl.BlockSpec((tm, n), lambda i: (i, 0))
    return pl.pallas_call(
        functools.partial(_merge_kernel, n_gc, gw),
        grid=(T // tm,),
        in_specs=[row(qa_w), row(qb_w), *gate_specs, row(D),
                  _resident(wa.shape), _resident(wb.shape), _resident(wo.shape), vec, vec],
        out_specs=[row(D), row(D)],
        out_shape=[jax.ShapeDtypeStruct((T, D), F32), jax.ShapeDtypeStruct((T, D), BF16)],
        compiler_params=_params(("parallel",)),
        name="merge_out",
    )(oa, ob, *([proj] * (2 * n_gc)), x, wa, wb, wo, gpost, gffn)


EPI_ROWS = 32


def _ffn_up_kernel(tm, seq_tiles, hm_ref, hp_ref, hn_ref, wa_ref, wb_ref, cwa_ref, cwb_ref,
                   cba_ref, cbb_ref, o_ref, hs_ref):
    i = pl.program_id(0)
    j = pl.program_id(1)
    halo = BF16_SUBLANES
    tn = o_ref.shape[1]
    epi_cols = min(tn, 2 * LANES)

    @pl.when(j == 0)
    def _():
        first = (i % seq_tiles) == 0
        last = (i % seq_tiles) == seq_tiles - 1
        hp = hp_ref[...]
        hn = hn_ref[...]
        hs_ref[0:halo, :] = jnp.where(first, jnp.zeros_like(hp), hp)
        hs_ref[halo:halo + tm, :] = hm_ref[...]
        hs_ref[halo + tm:, :] = jnp.where(last, jnp.zeros_like(hn), hn)

    def conv(u, cw_ref, cb_ref):
        cw = cw_ref[...]
        c = cb_ref[...] + u[halo - 1:halo - 1 + tm] * cw[0:1]
        c = c + u[halo:halo + tm] * cw[1:2]
        return c + u[halo + 1:halo + 1 + tm] * cw[2:3]

    ua = jnp.dot(hs_ref[...], wa_ref[...], preferred_element_type=F32)
    a = _gelu_tanh(conv(ua, cwa_ref, cba_ref))
    ub = jnp.dot(hs_ref[...], wb_ref[...], preferred_element_type=F32)
    o_ref[...] = (a * conv(ub, cwb_ref, cbb_ref)).astype(o_ref.dtype)


def _ffn_up(h2, w_up, conv_w, conv_b, seq):
    T, D = h2.shape
    F = w_up.shape[1] // 2
    tm = _pick(seq, (1024, 512, 256, 128))
    tn = _pick(F, (512, 256, 128))
    nb = F // tn
    halo = BF16_SUBLANES
    per = tm // halo
    n_halo_blocks = T // halo
    seq_tiles = seq // tm
    return pl.pallas_call(
        functools.partial(_ffn_up_kernel, tm, seq_tiles),
        grid=(T // tm, nb),
        in_specs=[pl.BlockSpec((tm, D), lambda i, j: (i, 0)),
                  pl.BlockSpec((halo, D), lambda i, j: (jnp.maximum(i * per - 1, 0), 0)),
                  pl.BlockSpec((halo, D), lambda i, j: (jnp.minimum((i + 1) * per, n_halo_blocks - 1), 0)),
                  pl.BlockSpec((D, tn), lambda i, j: (0, j)),
                  pl.BlockSpec((D, tn), lambda i, j: (0, nb + j)),
                  pl.BlockSpec((3, tn), lambda i, j: (0, j)),
                  pl.BlockSpec((3, tn), lambda i, j: (0, nb + j)),
                  pl.BlockSpec((1, tn), lambda i, j: (0, j)),
                  pl.BlockSpec((1, tn), lambda i, j: (0, nb + j))],
        out_specs=pl.BlockSpec((tm, tn), lambda i, j: (i, j)),
        out_shape=jax.ShapeDtypeStruct((T, F), BF16),
        scratch_shapes=[pltpu.VMEM((tm + 2 * halo, D), BF16)],
        compiler_params=_params(("parallel", "arbitrary")),
        name="ffn_up",
    )(h2, h2, h2, w_up, w_up, conv_w, conv_w, conv_b, conv_b)


def _ffn_down_kernel(g_ref, w_ref, x1_ref, gain_ref, o_ref):
    f = jnp.dot(g_ref[...], w_ref[...], preferred_element_type=F32)
    o_ref[...] = x1_ref[...] + _rms(f, gain_ref[...])


def _ffn_down(g, w_down, x1, gain):
    T, F = g.shape
    D = x1.shape[1]
    tm = _pick(T, (256, 128))
    return pl.pallas_call(
        _ffn_down_kernel,
        grid=(T // tm,),
        in_specs=[pl.BlockSpec((tm, F), lambda i: (i, 0)), _resident(w_down.shape),
                  pl.BlockSpec((tm, D), lambda i: (i, 0)), pl.BlockSpec((1, D), lambda i: (0, 0))],
        out_specs=pl.BlockSpec((tm, D), lambda i: (i, 0)),
        out_shape=jax.ShapeDtypeStruct((T, D), F32),
        compiler_params=_params(("parallel",)),
        name="ffn_down",
    )(g, w_down, x1, gain)


def _rope_angles(pos, dim):
    inv_freq = ROPE_THETA ** (-jnp.arange(0, dim, 2, dtype=F32) / dim)
    ang = pos.astype(F32)[:, None] * inv_freq[None, :]
    return jnp.concatenate([ang, ang], axis=-1)


def _rope_tables(seq):
    t = jnp.arange(seq, dtype=jnp.int32)
    half = HEAD_DIM // 2
    ang_a = jnp.concatenate([_rope_angles(t // GRID_W, half), _rope_angles(t % GRID_W, half)], axis=-1)
    lane = jnp.arange(HEAD_DIM)[None, :]
    low = (lane % half) < (half // 2)
    cos_a, sin_a = jnp.cos(ang_a), jnp.sin(ang_a)
    sa1 = jnp.where(low, -sin_a, 0.0)
    sa2 = jnp.where(low, 0.0, sin_a)
    ang_t = _rope_angles(t, HEAD_DIM)
    sin_t = jnp.where(lane < half, -jnp.sin(ang_t), jnp.sin(ang_t))
    return cos_a, sa1, sa2, jnp.cos(ang_t), sin_t


def _trunk(x, p):
    B, S, D = x.shape
    T = B * S
    qa_w = p["wa"].shape[0]
    qb_w = p["wb"].shape[0]
    n_cols = p["w_in"].shape[1]
    kv_w = (n_cols - 2 * D - qa_w - qb_w) // 4
    n_kv = kv_w // HEAD_DIM
    group = qa_w // kv_w
    assert qb_w == qa_w
    seg = (("qa", qa_w), ("ka", kv_w), ("va", kv_w), ("qb", qb_w), ("kb", kv_w), ("vb", kv_w),
           ("gate", 2 * D))
    col = {}
    off = 0
    for kind, n in seg:
        col[kind] = off
        off += n

    x2 = x.reshape(T, D)
    proj = _in_proj(x2, p["g_pre_mix"], p["w_in"], p["gq"], p["gk"], _rope_tables(S), S, seg)
    oa, ob = _attention(proj, p["sink"], B, S, n_kv, group, col)
    x1, h2 = _merge_out(oa, ob, proj, x2, p["wa"], p["wb"], p["wo"], p["g_post_mix"], p["g_pre_ffn"],
                        col["gate"])
    g = _ffn_up(h2, p["w_up"], p["conv_w"], p["conv_b"], S)
    y = _ffn_down(g, p["w_down"], x1, p["g_post_ffn"])
    return y.reshape(B, S, D)


def kernel(x_prompt, x_sample, norm_pre_mix, w_in, q_norm_a, k_norm_a, sink_b, w_branch_a, w_branch_b,
           w_out, norm_post_mix, norm_pre_ffn, w_up, conv_w, conv_b, w_down, norm_post_ffn):
    depth = w_in.shape[0]
    for l in range(depth):
        p = dict(
            g_pre_mix=norm_pre_mix[l][None, :], w_in=w_in[l].astype(BF16),
            gq=(q_norm_a[l] * Q_SCALE)[None, :], gk=k_norm_a[l][None, :],
            sink=sink_b[l], wa=w_branch_a[l].astype(BF16), wb=w_branch_b[l].astype(BF16),
            wo=w_out[l].astype(BF16), g_post_mix=norm_post_mix[l][None, :],
            g_pre_ffn=norm_pre_ffn[l][None, :], w_up=w_up[l].astype(BF16), conv_w=conv_w[l],
            conv_b=conv_b[l][None, :], w_down=w_down[l].astype(BF16),
            g_post_ffn=norm_post_ffn[l][None, :])
        x_prompt = _trunk(x_prompt, p)
        x_sample = _trunk(x_sample, p)
    return (x_prompt, x_sample)
```

```python
import functools
import math

import jax
import jax.numpy as jnp
from jax import lax
from jax.experimental import pallas as pl
from jax.experimental.pallas import tpu as pltpu

HEAD_DIM = 128
WINDOW = 128
GRID_W = 64
ROPE_THETA = 10000.0
EPS = 1e-6
LOG2E = 1.4426950408889634
Q_SCALE = HEAD_DIM ** -0.5 * LOG2E
LANES = 128
BF16_SUBLANES = 16
VMEM_LIMIT = 56 * 1024 * 1024

F32 = jnp.float32
BF16 = jnp.bfloat16


def _pick(n, candidates):
    for c in candidates:
        if n % c == 0:
            return c
    raise ValueError(f"no tile in {candidates} divides {n}")


def _rms(x, gain):
    r = lax.rsqrt(jnp.mean(x * x, axis=-1, keepdims=True) + EPS)
    return x * r * gain


def _gelu_tanh(x):
    k = math.sqrt(2.0 / math.pi)
    half = 0.5 * x
    return half + half * jnp.tanh(x * (k + (k * 0.044715) * (x * x)))


def _params(sem):
    return pltpu.CompilerParams(dimension_semantics=sem, vmem_limit_bytes=VMEM_LIMIT)


def _bufs(count, shape, dtype):
    return [pltpu.VMEM(shape, dtype) for _ in range(count)]


PIECE_BUFS = 3


def _in_proj_kernel(chunk_kind, piece, x_ref, g_ref, w_ref, gq_ref, gk_ref, cosa_ref, sa1_ref, sa2_ref,
                    cost_ref, sint_ref, o_ref, hs_ref, acc_bufs):
    hs_ref[...] = _rms(x_ref[...], g_ref[...]).astype(BF16)
    n_pieces = o_ref.shape[1] // piece

    def axial(h):
        return (h * cosa_ref[...] + pltpu.roll(h, 96, 1) * sa1_ref[...]
                + pltpu.roll(h, 32, 1) * sa2_ref[...])

    def rope1d(h):
        return h * cost_ref[...] + pltpu.roll(h, 64, 1) * sint_ref[...]

    def epilogue(kind, h):
        if kind == "qa":
            return axial(_rms(h, gq_ref[...]))
        if kind == "ka":
            return axial(_rms(h, gk_ref[...]))
        if kind == "qb":
            return rope1d(h) * Q_SCALE
        if kind == "kb":
            return rope1d(h)
        if kind == "gate":
            return 0.5 + 0.5 * jnp.tanh(0.5 * h)
        return h

    for t in range(n_pieces + 2):
        if t < n_pieces:
            acc_bufs[t % PIECE_BUFS][...] = jnp.dot(hs_ref[...], w_ref[:, t * piece:(t + 1) * piece],
                                                    preferred_element_type=F32)
        k = t - 2
        if k >= 0:
            acc_ref = acc_bufs[k % PIECE_BUFS]
            for c in range(piece // LANES):
                col = k * piece + c * LANES
                h = acc_ref[:, c * LANES:(c + 1) * LANES]
                o_ref[:, col:col + LANES] = epilogue(chunk_kind[col // LANES], h).astype(o_ref.dtype)


def _in_proj(x, gain, w, gq, gk, tabs, seq, seg_kinds):
    T, D = x.shape
    N = w.shape[1]
    tm = _pick(seq, (256, 128))
    piece = _pick(N, (512, 256, 128))
    chunk_kind = []
    for kind, n in seg_kinds:
        chunk_kind += [kind] * (n // LANES)
    seq_tiles = seq // tm
    tab_spec = pl.BlockSpec((tm, LANES), lambda i: (i % seq_tiles, 0))
    vec = lambda n: pl.BlockSpec((1, n), lambda i: (0, 0))
    return pl.pallas_call(
        functools.partial(_in_proj_kernel, tuple(chunk_kind), piece),
        grid=(T // tm,),
        in_specs=[pl.BlockSpec((tm, D), lambda i: (i, 0)), vec(D), _resident(w.shape), vec(LANES), vec(LANES),
                  tab_spec, tab_spec, tab_spec, tab_spec, tab_spec],
        out_specs=pl.BlockSpec((tm, N), lambda i: (i, 0)),
        out_shape=jax.ShapeDtypeStruct((T, N), BF16),
        scratch_shapes=[pltpu.VMEM((tm, D), BF16), _bufs(PIECE_BUFS, (tm, piece), F32)],
        compiler_params=_params(("parallel",)),
        name="in_proj",
    )(x, gain, w, gq, gk, *tabs)


def _unstack_heads(o_ref, o, group, tq):
    for g in range(group):
        o_ref[:, g * HEAD_DIM:(g + 1) * HEAD_DIM] = o[g * tq:(g + 1) * tq].astype(o_ref.dtype)


def _qk(q, k):
    return lax.dot_general(q, k, (((1,), (1,)), ((), ())), preferred_element_type=F32)


ROW_BLOCK = 64
DEPTH = 3


def _attn_kernel(group, tq, tk, span, sink_ref, bias_ref, qa_ref, ka_ref, va_ref, qb_ref, kb_ref, vb_ref,
                 oa_ref, ob_ref, s_bufs, p_bufs, alpha_bufs, m_ref, acc_ref, qs_ref, qw_ref, wden_ref):
    h = pl.program_id(1)
    qi = pl.program_id(2)
    seq = ka_ref.shape[0]
    rows = group * tq
    n = seq // tk
    for g in range(group):
        qs_ref[g * tq:(g + 1) * tq, :] = qa_ref[:, g * HEAD_DIM:(g + 1) * HEAD_DIM]
        qw_ref[g * tq:(g + 1) * tq, :] = qb_ref[:, g * HEAD_DIM:(g + 1) * HEAD_DIM]
    m_ref[...] = jnp.full(m_ref.shape, -jnp.inf, F32)
    acc_ref[...] = jnp.zeros(acc_ref.shape, F32)
    row_blocks = [slice(r, r + ROW_BLOCK) for r in range(0, rows, ROW_BLOCK)]

    def with_ones(v):
        return jnp.concatenate([v, jnp.ones_like(v)], axis=1)

    def exp_block(s_ref, p_ref, sl, m, width):
        p_ref[sl, :width] = jnp.concatenate(
            [jnp.exp2(s_ref[sl, j * LANES:(j + 1) * LANES] - m) for j in range(width // LANES)],
            axis=1).astype(BF16)

    start = pl.multiple_of(jnp.clip(qi * tq - WINDOW, 0, seq - span), LANES)

    def win_scores(buf):
        s_bufs[buf][:, :span] = _qk(qw_ref[...], kb_ref[pl.ds(start, span), :])

    def win_softmax(buf):
        s_ref, p_ref, m_w = s_bufs[buf], p_bufs[buf], alpha_bufs[buf]
        for sl in row_blocks:
            t = sl.start % tq
            sink = sink_ref[h * group + sl.start // tq] * LOG2E
            sb = s_ref[sl, :span] + bias_ref[t:t + ROW_BLOCK, :]
            s_ref[sl, :span] = sb
            m = jnp.maximum(jnp.max(sb, axis=-1, keepdims=True), sink)
            m_w[sl, :] = jnp.broadcast_to(m, (ROW_BLOCK, LANES))
            wden_ref[sl, :] = jnp.broadcast_to(jnp.exp2(sink - m), (ROW_BLOCK, LANES))
        for sl in row_blocks:
            exp_block(s_ref, p_ref, sl, m_w[sl, :], span)

    def win_weighted(buf):
        pv = jnp.dot(p_bufs[buf][:, :span], with_ones(vb_ref[pl.ds(start, span), :]),
                     preferred_element_type=F32)
        _unstack_heads(ob_ref, pv[:, :HEAD_DIM] / (pv[:, HEAD_DIM:] + wden_ref[...]), group, tq)

    def chunk(ref, c):
        first = c * tk if isinstance(c, int) else pl.multiple_of(c * tk, tk)
        return ref[pl.ds(first, tk), :]

    def scores(c, buf):
        s_bufs[buf][:, :tk] = _qk(qs_ref[...], chunk(ka_ref, c))

    def softmax(buf):
        s_ref, p_ref, alpha_ref = s_bufs[buf], p_bufs[buf], alpha_bufs[buf]
        for sl in row_blocks:
            m_old = m_ref[sl, :]
            m_new = jnp.maximum(m_old, jnp.max(s_ref[sl, :tk], axis=-1, keepdims=True))
            alpha_ref[sl, :] = jnp.exp2(m_old - m_new)
            m_ref[sl, :] = m_new
        for sl in row_blocks:
            exp_block(s_ref, p_ref, sl, m_ref[sl, :], tk)

    def weighted(c, buf):
        pv = jnp.dot(p_bufs[buf][:, :tk], with_ones(chunk(va_ref, c)), preferred_element_type=F32)
        for sl in row_blocks:
            alpha = alpha_bufs[buf][sl, :]
            acc_ref[sl, :] = jnp.concatenate([alpha, alpha], axis=1) * acc_ref[sl, :] + pv[sl]

    n_items = n + 1

    def stage_scores(item, buf):
        if isinstance(item, int) and item == 0:
            win_scores(buf)
        else:
            scores(item - 1, buf)

    def stage_softmax(item, buf):
        if isinstance(item, int) and item == 0:
            win_softmax(buf)
        else:
            softmax(buf)

    def stage_weighted(item, buf):
        if isinstance(item, int) and item == 0:
            win_weighted(buf)
        else:
            weighted(item - 1, buf)

    def step(t, phase):
        static = isinstance(t, int)
        if not static or 0 <= t < n_items:
            stage_softmax(t, phase)
        if not static or 0 <= t + 1 < n_items:
            stage_scores(t + 1, (phase + 1) % DEPTH)
        if not static or 0 <= t - 2 < n_items:
            stage_weighted(t - 2, (phase - 2) % DEPTH)

    for t in range(-1, n_items + 2):
        step(t, t % DEPTH)
    acc = acc_ref[...]
    _unstack_heads(oa_ref, acc[:, :HEAD_DIM] / acc[:, HEAD_DIM:], group, tq)


def _band_bias(tq, span):
    r = jnp.arange(tq)[:, None]
    c = jnp.arange(span)[None, :]
    shifts = (0, -WINDOW, tq - span)
    return jnp.stack([jnp.where(jnp.abs(c + d - r) <= WINDOW, 0.0, -jnp.inf).astype(F32) for d in shifts])


def _attention(proj, sink, batch, seq, n_kv, group, col):
    T = proj.shape[0]
    tq = _pick(seq, (256, 128))
    tk = _pick(seq, (512, 256, 128))
    span = min(tq + 2 * WINDOW, seq)
    assert (seq // tk) % 2 == 0
    width = max(tk, span)
    nq = seq // tq
    qw = group * HEAD_DIM
    rows = group * tq
    kind = lambda i: jnp.where(i == 0, 0, jnp.where(i == nq - 1, 2, 1))
    q_spec = lambda c: pl.BlockSpec((tq, qw), lambda b, h, i: (b * nq + i, c // qw + h))
    kv_spec = lambda c: pl.BlockSpec((seq, HEAD_DIM), lambda b, h, i: (b, c // HEAD_DIM + h))
    o_spec = pl.BlockSpec((tq, qw), lambda b, h, i: (b * nq + i, h))
    o_shape = jax.ShapeDtypeStruct((T, n_kv * qw), BF16)
    return pl.pallas_call(
        functools.partial(_attn_kernel, group, tq, tk, span),
        grid=(batch, n_kv, nq),
        in_specs=[pl.BlockSpec(memory_space=pltpu.SMEM),
                  pl.BlockSpec((None, tq, span), lambda b, h, i: (kind(i), 0, 0)),
                  q_spec(col["qa"]), kv_spec(col["ka"]), kv_spec(col["va"]),
                  q_spec(col["qb"]), kv_spec(col["kb"]), kv_spec(col["vb"])],
        out_specs=[o_spec, o_spec],
        out_shape=[o_shape, o_shape],
        scratch_shapes=[_bufs(DEPTH, (rows, width), F32), _bufs(DEPTH, (rows, width), BF16),
                        _bufs(DEPTH, (rows, LANES), F32),
                        pltpu.VMEM((rows, LANES), F32), pltpu.VMEM((rows, 2 * HEAD_DIM), F32),
                        pltpu.VMEM((rows, HEAD_DIM), BF16), pltpu.VMEM((rows, HEAD_DIM), BF16),
                        pltpu.VMEM((rows, LANES), F32)],
        compiler_params=_params(("parallel", "parallel", "arbitrary")),
        name="attention",
    )(sink, _band_bias(tq, span), proj, proj, proj, proj, proj, proj)


def _merge_kernel(n_gc, gw, *refs):
    oa_ref, ob_ref = refs[0], refs[1]
    ga_refs = refs[2:2 + n_gc]
    gb_refs = refs[2 + n_gc:2 + 2 * n_gc]
    x_ref, wa_ref, wb_ref, wo_ref, gpost_ref, gffn_ref, x1_ref, h2_ref = refs[2 + 2 * n_gc:]
    oa = oa_ref[...]
    ob = ob_ref[...]
    parts = []
    for c in range(n_gc):
        sl = slice(c * gw, (c + 1) * gw)
        a = jnp.dot(oa, wa_ref[:, sl], preferred_element_type=F32)
        b = jnp.dot(ob, wb_ref[:, sl], preferred_element_type=F32)
        merged = ga_refs[c][...].astype(F32) * a + gb_refs[c][...].astype(F32) * b
        parts.append(merged.astype(BF16))
    merged = jnp.concatenate(parts, axis=1)
    mix = jnp.dot(merged, wo_ref[...], preferred_element_type=F32)
    x1 = x_ref[...] + _rms(mix, gpost_ref[...])
    x1_ref[...] = x1
    h2_ref[...] = _rms(x1, gffn_ref[...]).astype(BF16)


def _resident(shape):
    return pl.BlockSpec(shape, lambda i: (0, 0), pipeline_mode=pl.Buffered(1))


def _merge_out(oa, ob, proj, x, wa, wb, wo, gpost, gffn, gate_col):
    T, D = x.shape
    qa_w, qb_w = oa.shape[1], ob.shape[1]
    tm = _pick(T, (256, 128))
    gw = math.gcd(gate_col, D)
    n_gc = D // gw
    gate_specs = [pl.BlockSpec((tm, gw), functools.partial(lambda i, c: (i, c), c=gate_col // gw + c))
                  for c in range(2 * n_gc)]
    vec = pl.BlockSpec((1, D), lambda i: (0, 0))
    row = lambda n: pl.BlockSpec((tm, n), lambda i: (i, 0))
    return pl.pallas_call(
        functools.partial(_merge_kernel, n_gc, gw),
        grid=(T // tm,),
        in_specs=[row(qa_w), row(qb_w), *gate_specs, row(D),
                  _resident(wa.shape), _resident(wb.shape), _resident(wo.shape), vec, vec],
        out_specs=[row(D), row(D)],
        out_shape=[jax.ShapeDtypeStruct((T, D), F32), jax.ShapeDtypeStruct((T, D), BF16)],
        compiler_params=_params(("parallel",)),
        name="merge_out",
    )(oa, ob, *([proj] * (2 * n_gc)), x, wa, wb, wo, gpost, gffn)


EPI_ROWS = 32


def _ffn_up_kernel(tm, seq_tiles, hm_ref, hp_ref, hn_ref, wa_ref, wb_ref, cwa_ref, cwb_ref,
                   cba_ref, cbb_ref, o_ref, hs_ref):
    i = pl.program_id(0)
    j = pl.program_id(1)
    halo = BF16_SUBLANES
    tn = o_ref.shape[1]
    epi_cols = min(tn, 2 * LANES)

    @pl.when(j == 0)
    def _():
        first = (i % seq_tiles) == 0
        last = (i % seq_tiles) == seq_tiles - 1
        hp = hp_ref[...]
        hn = hn_ref[...]
        hs_ref[0:halo, :] = jnp.where(first, jnp.zeros_like(hp), hp)
        hs_ref[halo:halo + tm, :] = hm_ref[...]
        hs_ref[halo + tm:, :] = jnp.where(last, jnp.zeros_like(hn), hn)

    def conv(u, cw_ref, cb_ref):
        cw = cw_ref[...]
        c = cb_ref[...] + u[halo - 1:halo - 1 + tm] * cw[0:1]
        c = c + u[halo:halo + tm] * cw[1:2]
        return c + u[halo + 1:halo + 1 + tm] * cw[2:3]

    ua = jnp.dot(hs_ref[...], wa_ref[...], preferred_element_type=F32)
    ub = jnp.dot(hs_ref[...], wb_ref[...], preferred_element_type=F32)
    a = _gelu_tanh(conv(ua, cwa_ref, cba_ref))
    o_ref[...] = (a * conv(ub, cwb_ref, cbb_ref)).astype(o_ref.dtype)


def _ffn_up(h2, w_up, conv_w, conv_b, seq):
    T, D = h2.shape
    F = w_up.shape[1] // 2
    tm = _pick(seq, (1024, 512, 256, 128))
    tn = _pick(F, (512, 256, 128))
    nb = F // tn
    halo = BF16_SUBLANES
    per = tm // halo
    n_halo_blocks = T // halo
    seq_tiles = seq // tm
    return pl.pallas_call(
        functools.partial(_ffn_up_kernel, tm, seq_tiles),
        grid=(T // tm, nb),
        in_specs=[pl.BlockSpec((tm, D), lambda i, j: (i, 0)),
                  pl.BlockSpec((halo, D), lambda i, j: (jnp.maximum(i * per - 1, 0), 0)),
                  pl.BlockSpec((halo, D), lambda i, j: (jnp.minimum((i + 1) * per, n_halo_blocks - 1), 0)),
                  pl.BlockSpec((D, tn), lambda i, j: (0, j)),
                  pl.BlockSpec((D, tn), lambda i, j: (0, nb + j)),
                  pl.BlockSpec((3, tn), lambda i, j: (0, j)),
                  pl.BlockSpec((3, tn), lambda i, j: (0, nb + j)),
                  pl.BlockSpec((1, tn), lambda i, j: (0, j)),
                  pl.BlockSpec((1, tn), lambda i, j: (0, nb + j))],
        out_specs=pl.BlockSpec((tm, tn), lambda i, j: (i, j)),
        out_shape=jax.ShapeDtypeStruct((T, F), BF16),
        scratch_shapes=[pltpu.VMEM((tm + 2 * halo, D), BF16)],
        compiler_params=_params(("parallel", "arbitrary")),
        name="ffn_up",
    )(h2, h2, h2, w_up, w_up, conv_w, conv_w, conv_b, conv_b)


def _ffn_down_kernel(g_ref, w_ref, x1_ref, gain_ref, o_ref):
    f = jnp.dot(g_ref[...], w_ref[...], preferred_element_type=F32)
    o_ref[...] = x1_ref[...] + _rms(f, gain_ref[...])


def _ffn_down(g, w_down, x1, gain):
    T, F = g.shape
    D = x1.shape[1]
    tm = _pick(T, (256, 128))
    return pl.pallas_call(
        _ffn_down_kernel,
        grid=(T // tm,),
        in_specs=[pl.BlockSpec((tm, F), lambda i: (i, 0)), _resident(w_down.shape),
                  pl.BlockSpec((tm, D), lambda i: (i, 0)), pl.BlockSpec((1, D), lambda i: (0, 0))],
        out_specs=pl.BlockSpec((tm, D), lambda i: (i, 0)),
        out_shape=jax.ShapeDtypeStruct((T, D), F32),
        compiler_params=_params(("parallel",)),
        name="ffn_down",
    )(g, w_down, x1, gain)


def _rope_angles(pos, dim):
    inv_freq = ROPE_THETA ** (-jnp.arange(0, dim, 2, dtype=F32) / dim)
    ang = pos.astype(F32)[:, None] * inv_freq[None, :]
    return jnp.concatenate([ang, ang], axis=-1)


def _rope_tables(seq):
    t = jnp.arange(seq, dtype=jnp.int32)
    half = HEAD_DIM // 2
    ang_a = jnp.concatenate([_rope_angles(t // GRID_W, half), _rope_angles(t % GRID_W, half)], axis=-1)
    lane = jnp.arange(HEAD_DIM)[None, :]
    low = (lane % half) < (half // 2)
    cos_a, sin_a = jnp.cos(ang_a), jnp.sin(ang_a)
    sa1 = jnp.where(low, -sin_a, 0.0)
    sa2 = jnp.where(low, 0.0, sin_a)
    ang_t = _rope_angles(t, HEAD_DIM)
    sin_t = jnp.where(lane < half, -jnp.sin(ang_t), jnp.sin(ang_t))
    return cos_a, sa1, sa2, jnp.cos(ang_t), sin_t


def _trunk(x, p):
    B, S, D = x.shape
    T = B * S
    qa_w = p["wa"].shape[0]
    qb_w = p["wb"].shape[0]
    n_cols = p["w_in"].shape[1]
    kv_w = (n_cols - 2 * D - qa_w - qb_w) // 4
    n_kv = kv_w // HEAD_DIM
    group = qa_w // kv_w
    assert qb_w == qa_w
    seg = (("qa", qa_w), ("ka", kv_w), ("va", kv_w), ("qb", qb_w), ("kb", kv_w), ("vb", kv_w),
           ("gate", 2 * D))
    col = {}
    off = 0
    for kind, n in seg:
        col[kind] = off
        off += n

    x2 = x.reshape(T, D)
    proj = _in_proj(x2, p["g_pre_mix"], p["w_in"], p["gq"], p["gk"], _rope_tables(S), S, seg)
    oa, ob = _attention(proj, p["sink"], B, S, n_kv, group, col)
    x1, h2 = _merge_out(oa, ob, proj, x2, p["wa"], p["wb"], p["wo"], p["g_post_mix"], p["g_pre_ffn"],
                        col["gate"])
    g = _ffn_up(h2, p["w_up"], p["conv_w"], p["conv_b"], S)
    y = _ffn_down(g, p["w_down"], x1, p["g_post_ffn"])
    return y.reshape(B, S, D)


def kernel(x_prompt, x_sample, norm_pre_mix, w_in, q_norm_a, k_norm_a, sink_b, w_branch_a, w_branch_b,
           w_out, norm_post_mix, norm_pre_ffn, w_up, conv_w, conv_b, w_down, norm_post_ffn):
    depth = w_in.shape[0]
    for l in range(depth):
        p = dict(
            g_pre_mix=norm_pre_mix[l][None, :], w_in=w_in[l].astype(BF16),
            gq=(q_norm_a[l] * Q_SCALE)[None, :], gk=k_norm_a[l][None, :],
            sink=sink_b[l], wa=w_branch_a[l].astype(BF16), wb=w_branch_b[l].astype(BF16),
            wo=w_out[l].astype(BF16), g_post_mix=norm_post_mix[l][None, :],
            g_pre_ffn=norm_pre_ffn[l][None, :], w_up=w_up[l].astype(BF16), conv_w=conv_w[l],
            conv_b=conv_b[l][None, :], w_down=w_down[l].astype(BF16),
            g_post_ffn=norm_post_ffn[l][None, :])
        x_prompt = _trunk(x_prompt, p)
        x_sample = _trunk(x_sample, p)
    return (x_prompt, x_sample)
```

```python
import functools
import math

import jax
import jax.numpy as jnp
from jax import lax
from jax.experimental import pallas as pl
from jax.experimental.pallas import tpu as pltpu

HEAD_DIM = 128
WINDOW = 128
GRID_W = 64
ROPE_THETA = 10000.0
EPS = 1e-6
LOG2E = 1.4426950408889634
Q_SCALE = HEAD_DIM ** -0.5 * LOG2E
LANES = 128
BF16_SUBLANES = 16
VMEM_LIMIT = 56 * 1024 * 1024

F32 = jnp.float32
BF16 = jnp.bfloat16


def _pick(n, candidates):
    for c in candidates:
        if n % c == 0:
            return c
    raise ValueError(f"no tile in {candidates} divides {n}")


def _rms(x, gain):
    r = lax.rsqrt(jnp.mean(x * x, axis=-1, keepdims=True) + EPS)
    return x * r * gain


def _gelu_tanh(x):
    k = math.sqrt(2.0 / math.pi)
    half = 0.5 * x
    return half + half * jnp.tanh(x * (k + (k * 0.044715) * (x * x)))


def _params(sem):
    return pltpu.CompilerParams(dimension_semantics=sem, vmem_limit_bytes=VMEM_LIMIT)


def _bufs(count, shape, dtype):
    return [pltpu.VMEM(shape, dtype) for _ in range(count)]


PIECE_BUFS = 3


def _in_proj_kernel(chunk_kind, piece, x_ref, g_ref, w_ref, gq_ref, gk_ref, cosa_ref, sa1_ref, sa2_ref,
                    cost_ref, sint_ref, o_ref, hs_ref, acc_bufs):
    hs_ref[...] = _rms(x_ref[...], g_ref[...]).astype(BF16)
    n_pieces = o_ref.shape[1] // piece

    half, quarter = HEAD_DIM // 2, HEAD_DIM // 4

    def axial(h):
        return (h * cosa_ref[...] + pltpu.roll(h, HEAD_DIM - quarter, 1) * sa1_ref[...]
                + pltpu.roll(h, quarter, 1) * sa2_ref[...])

    def rope1d(h):
        return h * cost_ref[...] + pltpu.roll(h, half, 1) * sint_ref[...]

    def epilogue(kind, h):
        if kind == "qa":
            return axial(_rms(h, gq_ref[...]))
        if kind == "ka":
            return axial(_rms(h, gk_ref[...]))
        if kind == "qb":
            return rope1d(h) * Q_SCALE
        if kind == "kb":
            return rope1d(h)
        if kind == "gate":
            return 0.5 + 0.5 * jnp.tanh(0.5 * h)
        return h

    for t in range(n_pieces + 2):
        if t < n_pieces:
            acc_bufs[t % PIECE_BUFS][...] = jnp.dot(hs_ref[...], w_ref[:, t * piece:(t + 1) * piece],
                                                    preferred_element_type=F32)
        k = t - 2
        if k >= 0:
            acc_ref = acc_bufs[k % PIECE_BUFS]
            for c in range(piece // LANES):
                col = k * piece + c * LANES
                h = acc_ref[:, c * LANES:(c + 1) * LANES]
                o_ref[:, col:col + LANES] = epilogue(chunk_kind[col // LANES], h).astype(o_ref.dtype)


def _in_proj(x, gain, w, gq, gk, tabs, seq, seg_kinds):
    T, D = x.shape
    N = w.shape[1]
    tm = _pick(seq, (256, 128))
    piece = _pick(N, (512, 256, 128))
    chunk_kind = []
    for kind, n in seg_kinds:
        chunk_kind += [kind] * (n // LANES)
    seq_tiles = seq // tm
    tab_spec = pl.BlockSpec((tm, LANES), lambda i: (i % seq_tiles, 0))
    vec = lambda n: pl.BlockSpec((1, n), lambda i: (0, 0))
    return pl.pallas_call(
        functools.partial(_in_proj_kernel, tuple(chunk_kind), piece),
        grid=(T // tm,),
        in_specs=[pl.BlockSpec((tm, D), lambda i: (i, 0)), vec(D), _resident(w.shape), vec(LANES), vec(LANES),
                  tab_spec, tab_spec, tab_spec, tab_spec, tab_spec],
        out_specs=pl.BlockSpec((tm, N), lambda i: (i, 0)),
        out_shape=jax.ShapeDtypeStruct((T, N), BF16),
        scratch_shapes=[pltpu.VMEM((tm, D), BF16), _bufs(PIECE_BUFS, (tm, piece), F32)],
        compiler_params=_params(("parallel",)),
        name="in_proj",
    )(x, gain, w, gq, gk, *tabs)


def _unstack_heads(o_ref, o, group, tq):
    for g in range(group):
        o_ref[:, g * HEAD_DIM:(g + 1) * HEAD_DIM] = o[g * tq:(g + 1) * tq].astype(o_ref.dtype)


def _qk(q, k):
    return lax.dot_general(q, k, (((1,), (1,)), ((), ())), preferred_element_type=F32)


ROW_BLOCK = 64
DEPTH = 3


def _attn_kernel(group, tq, tk, span, sink_ref, bias_ref, qa_ref, ka_ref, va_ref, qb_ref, kb_ref, vb_ref,
                 oa_ref, ob_ref, s_bufs, p_bufs, alpha_bufs, m_ref, acc_ref, qs_ref, qw_ref, wden_ref):
    h = pl.program_id(1)
    qi = pl.program_id(2)
    seq = ka_ref.shape[0]
    rows = group * tq
    n = seq // tk
    for g in range(group):
        qs_ref[g * tq:(g + 1) * tq, :] = qa_ref[:, g * HEAD_DIM:(g + 1) * HEAD_DIM]
        qw_ref[g * tq:(g + 1) * tq, :] = qb_ref[:, g * HEAD_DIM:(g + 1) * HEAD_DIM]
    m_ref[...] = jnp.full(m_ref.shape, -jnp.inf, F32)
    acc_ref[...] = jnp.zeros(acc_ref.shape, F32)
    row_blocks = [slice(r, r + ROW_BLOCK) for r in range(0, rows, ROW_BLOCK)]

    def with_ones(v):
        return jnp.concatenate([v, jnp.ones_like(v)], axis=1)

    def exp_block(s_ref, p_ref, sl, m, width):
        p_ref[sl, :width] = jnp.concatenate(
            [jnp.exp2(s_ref[sl, j * LANES:(j + 1) * LANES] - m) for j in range(width // LANES)],
            axis=1).astype(BF16)

    start = pl.multiple_of(jnp.clip(qi * tq - WINDOW, 0, seq - span), LANES)

    def win_scores(buf):
        s_bufs[buf][:, :span] = _qk(qw_ref[...], kb_ref[pl.ds(start, span), :])

    def win_softmax(buf):
        s_ref, p_ref, m_w = s_bufs[buf], p_bufs[buf], alpha_bufs[buf]
        for sl in row_blocks:
            t = sl.start % tq
            sink = sink_ref[h * group + sl.start // tq] * LOG2E
            sb = s_ref[sl, :span] + bias_ref[t:t + ROW_BLOCK, :]
            s_ref[sl, :span] = sb
            m = jnp.maximum(jnp.max(sb, axis=-1, keepdims=True), sink)
            m_w[sl, :] = jnp.broadcast_to(m, (ROW_BLOCK, LANES))
            wden_ref[sl, :] = jnp.broadcast_to(jnp.exp2(sink - m), (ROW_BLOCK, LANES))
        for sl in row_blocks:
            exp_block(s_ref, p_ref, sl, m_w[sl, :], span)

    def win_weighted(buf):
        pv = jnp.dot(p_bufs[buf][:, :span], with_ones(vb_ref[pl.ds(start, span), :]),
                     preferred_element_type=F32)
        _unstack_heads(ob_ref, pv[:, :HEAD_DIM] / (pv[:, HEAD_DIM:] + wden_ref[...]), group, tq)

    def chunk(ref, c):
        first = c * tk if isinstance(c, int) else pl.multiple_of(c * tk, tk)
        return ref[pl.ds(first, tk), :]

    def scores(c, buf):
        s_bufs[buf][:, :tk] = _qk(qs_ref[...], chunk(ka_ref, c))

    def softmax(buf):
        s_ref, p_ref, alpha_ref = s_bufs[buf], p_bufs[buf], alpha_bufs[buf]
        for sl in row_blocks:
            m_old = m_ref[sl, :]
            m_new = jnp.maximum(m_old, jnp.max(s_ref[sl, :tk], axis=-1, keepdims=True))
            alpha_ref[sl, :] = jnp.exp2(m_old - m_new)
            m_ref[sl, :] = m_new
        for sl in row_blocks:
            exp_block(s_ref, p_ref, sl, m_ref[sl, :], tk)

    def weighted(c, buf):
        pv = jnp.dot(p_bufs[buf][:, :tk], with_ones(chunk(va_ref, c)), preferred_element_type=F32)
        for sl in row_blocks:
            alpha = alpha_bufs[buf][sl, :]
            acc_ref[sl, :] = jnp.concatenate([alpha, alpha], axis=1) * acc_ref[sl, :] + pv[sl]

    n_items = n + 1

    def stage_scores(item, buf):
        if isinstance(item, int) and item == 0:
            win_scores(buf)
        else:
            scores(item - 1, buf)

    def stage_softmax(item, buf):
        if isinstance(item, int) and item == 0:
            win_softmax(buf)
        else:
            softmax(buf)

    def stage_weighted(item, buf):
        if isinstance(item, int) and item == 0:
            win_weighted(buf)
        else:
            weighted(item - 1, buf)

    def step(t, phase):
        static = isinstance(t, int)
        if not static or 0 <= t < n_items:
            stage_softmax(t, phase)
        if not static or 0 <= t + 1 < n_items:
            stage_scores(t + 1, (phase + 1) % DEPTH)
        if not static or 0 <= t - 2 < n_items:
            stage_weighted(t - 2, (phase - 2) % DEPTH)

    for t in range(-1, n_items + 2):
        step(t, t % DEPTH)
    acc = acc_ref[...]
    _unstack_heads(oa_ref, acc[:, :HEAD_DIM] / acc[:, HEAD_DIM:], group, tq)


def _band_bias(tq, span):
    r = jnp.arange(tq)[:, None]
    c = jnp.arange(span)[None, :]
    shifts = (0, -WINDOW, tq - span)
    return jnp.stack([jnp.where(jnp.abs(c + d - r) <= WINDOW, 0.0, -jnp.inf).astype(F32) for d in shifts])


def _attention(proj, sink, batch, seq, n_kv, group, col):
    T = proj.shape[0]
    tq = _pick(seq, (256, 128))
    tk = _pick(seq, (512, 256, 128))
    if seq // tk <= 4:
        tk = _pick(seq // 2, (1024, 512, 256, 128))
    span = min(tq + 2 * WINDOW, seq)
    assert (seq // tk) % 2 == 0
    width = max(tk, span)
    nq = seq // tq
    qw = group * HEAD_DIM
    rows = group * tq
    kind = lambda i: jnp.where(i == 0, 0, jnp.where(i == nq - 1, 2, 1))
    q_spec = lambda c: pl.BlockSpec((tq, qw), lambda b, h, i: (b * nq + i, c // qw + h))
    kv_spec = lambda c: pl.BlockSpec((seq, HEAD_DIM), lambda b, h, i: (b, c // HEAD_DIM + h))
    o_spec = pl.BlockSpec((tq, qw), lambda b, h, i: (b * nq + i, h))
    o_shape = jax.ShapeDtypeStruct((T, n_kv * qw), BF16)
    return pl.pallas_call(
        functools.partial(_attn_kernel, group, tq, tk, span),
        grid=(batch, n_kv, nq),
        in_specs=[pl.BlockSpec(memory_space=pltpu.SMEM),
                  pl.BlockSpec((None, tq, span), lambda b, h, i: (kind(i), 0, 0)),
                  q_spec(col["qa"]), kv_spec(col["ka"]), kv_spec(col["va"]),
                  q_spec(col["qb"]), kv_spec(col["kb"]), kv_spec(col["vb"])],
        out_specs=[o_spec, o_spec],
        out_shape=[o_shape, o_shape],
        scratch_shapes=[_bufs(DEPTH, (rows, width), F32), _bufs(DEPTH, (rows, width), BF16),
                        _bufs(DEPTH, (rows, LANES), F32),
                        pltpu.VMEM((rows, LANES), F32), pltpu.VMEM((rows, 2 * HEAD_DIM), F32),
                        pltpu.VMEM((rows, HEAD_DIM), BF16), pltpu.VMEM((rows, HEAD_DIM), BF16),
                        pltpu.VMEM((rows, LANES), F32)],
        compiler_params=_params(("parallel", "parallel", "arbitrary")),
        name="attention",
    )(sink, _band_bias(tq, span), proj, proj, proj, proj, proj, proj)


def _merge_kernel(n_gc, gw, *refs):
    oa_ref, ob_ref = refs[0], refs[1]
    ga_refs = refs[2:2 + n_gc]
    gb_refs = refs[2 + n_gc:2 + 2 * n_gc]
    x_ref, wa_ref, wb_ref, wo_ref, gpost_ref, gffn_ref, x1_ref, h2_ref = refs[2 + 2 * n_gc:]
    oa = oa_ref[...]
    ob = ob_ref[...]
    parts = []
    for c in range(n_gc):
        sl = slice(c * gw, (c + 1) * gw)
        a = jnp.dot(oa, wa_ref[:, sl], preferred_element_type=F32)
        b = jnp.dot(ob, wb_ref[:, sl], preferred_element_type=F32)
        merged = ga_refs[c][...].astype(F32) * a + gb_refs[c][...].astype(F32) * b
        parts.append(merged.astype(BF16))
    merged = jnp.concatenate(parts, axis=1)
    mix = jnp.dot(merged, wo_ref[...], preferred_element_type=F32)
    x1 = x_ref[...] + _rms(mix, gpost_ref[...])
    x1_ref[...] = x1
    h2_ref[...] = _rms(x1, gffn_ref[...]).astype(BF16)


def _resident(shape):
    return pl.BlockSpec(shape, lambda i: (0, 0), pipeline_mode=pl.Buffered(1))


def _merge_out(oa, ob, proj, x, wa, wb, wo, gpost, gffn, gate_col):
    T, D = x.shape
    qa_w, qb_w = oa.shape[1], ob.shape[1]
    tm = _pick(T, (256, 128))
    gw = math.gcd(gate_col, D)
    n_gc = D // gw
    gate_specs = [pl.BlockSpec((tm, gw), functools.partial(lambda i, c: (i, c), c=gate_col // gw + c))
                  for c in range(2 * n_gc)]
    vec = pl.BlockSpec((1, D), lambda i: (0, 0))
    row = lambda n: pl.BlockSpec((tm, n), lambda i: (i, 0))
    return pl.pallas_call(
        functools.partial(_merge_kernel, n_gc, gw),
        grid=(T // tm,),
        in_specs=[row(qa_w), row(qb_w), *gate_specs, row(D),
                  _resident(wa.shape), _resident(wb.shape), _resident(wo.shape), vec, vec],
        out_specs=[row(D), row(D)],
        out_shape=[jax.ShapeDtypeStruct((T, D), F32), jax.ShapeDtypeStruct((T, D), BF16)],
        compiler_params=_params(("parallel",)),
        name="merge_out",
    )(oa, ob, *([proj] * (2 * n_gc)), x, wa, wb, wo, gpost, gffn)


def _ffn_up_kernel(tm, seq_tiles, hm_ref, hp_ref, hn_ref, wa_ref, wb_ref, cwa_ref, cwb_ref,
                   cba_ref, cbb_ref, o_ref, hs_ref):
    i = pl.program_id(0)
    j = pl.program_id(1)
    halo = BF16_SUBLANES

    @pl.when(j == 0)
    def _():
        first = (i % seq_tiles) == 0
        last = (i % seq_tiles) == seq_tiles - 1
        hp = hp_ref[...]
        hn = hn_ref[...]
        hs_ref[0:halo, :] = jnp.where(first, jnp.zeros_like(hp), hp)
        hs_ref[halo:halo + tm, :] = hm_ref[...]
        hs_ref[halo + tm:, :] = jnp.where(last, jnp.zeros_like(hn), hn)

    def conv(u, cw_ref, cb_ref):
        cw = cw_ref[...]
        c = cb_ref[...] + u[halo - 1:halo - 1 + tm] * cw[0:1]
        c = c + u[halo:halo + tm] * cw[1:2]
        return c + u[halo + 1:halo + 1 + tm] * cw[2:3]

    ua = jnp.dot(hs_ref[...], wa_ref[...], preferred_element_type=F32)
    ub = jnp.dot(hs_ref[...], wb_ref[...], preferred_element_type=F32)
    a = _gelu_tanh(conv(ua, cwa_ref, cba_ref))
    o_ref[...] = (a * conv(ub, cwb_ref, cbb_ref)).astype(o_ref.dtype)


def _ffn_up(h2, w_up, conv_w, conv_b, seq):
    T, D = h2.shape
    F = w_up.shape[1] // 2
    tm = _pick(seq, (1024, 512, 256, 128))
    tn = _pick(F, (512, 256, 128))
    nb = F // tn
    halo = BF16_SUBLANES
    per = tm // halo
    n_halo_blocks = T // halo
    seq_tiles = seq // tm
    return pl.pallas_call(
        functools.partial(_ffn_up_kernel, tm, seq_tiles),
        grid=(T // tm, nb),
        in_specs=[pl.BlockSpec((tm, D), lambda i, j: (i, 0)),
                  pl.BlockSpec((halo, D), lambda i, j: (jnp.maximum(i * per - 1, 0), 0)),
                  pl.BlockSpec((halo, D), lambda i, j: (jnp.minimum((i + 1) * per, n_halo_blocks - 1), 0)),
                  pl.BlockSpec((D, tn), lambda i, j: (0, j)),
                  pl.BlockSpec((D, tn), lambda i, j: (0, nb + j)),
                  pl.BlockSpec((3, tn), lambda i, j: (0, j)),
                  pl.BlockSpec((3, tn), lambda i, j: (0, nb + j)),
                  pl.BlockSpec((1, tn), lambda i, j: (0, j)),
                  pl.BlockSpec((1, tn), lambda i, j: (0, nb + j))],
        out_specs=pl.BlockSpec((tm, tn), lambda i, j: (i, j)),
        out_shape=jax.ShapeDtypeStruct((T, F), BF16),
        scratch_shapes=[pltpu.VMEM((tm + 2 * halo, D), BF16)],
        compiler_params=_params(("parallel", "arbitrary")),
        name="ffn_up",
    )(h2, h2, h2, w_up, w_up, conv_w, conv_w, conv_b, conv_b)


def _ffn_down_kernel(g_ref, w_ref, x1_ref, gain_ref, o_ref):
    f = jnp.dot(g_ref[...], w_ref[...], preferred_element_type=F32)
    o_ref[...] = x1_ref[...] + _rms(f, gain_ref[...])


def _ffn_down(g, w_down, x1, gain):
    T, F = g.shape
    D = x1.shape[1]
    tm = _pick(T, (256, 128))
    return pl.pallas_call(
        _ffn_down_kernel,
        grid=(T // tm,),
        in_specs=[pl.BlockSpec((tm, F), lambda i: (i, 0)), _resident(w_down.shape),
                  pl.BlockSpec((tm, D), lambda i: (i, 0)), pl.BlockSpec((1, D), lambda i: (0, 0))],
        out_specs=pl.BlockSpec((tm, D), lambda i: (i, 0)),
        out_shape=jax.ShapeDtypeStruct((T, D), F32),
        compiler_params=_params(("parallel",)),
        name="ffn_down",
    )(g, w_down, x1, gain)


def _rope_angles(pos, dim):
    inv_freq = ROPE_THETA ** (-jnp.arange(0, dim, 2, dtype=F32) / dim)
    ang = pos.astype(F32)[:, None] * inv_freq[None, :]
    return jnp.concatenate([ang, ang], axis=-1)


def _rope_tables(seq):
    t = jnp.arange(seq, dtype=jnp.int32)
    half = HEAD_DIM // 2
    ang_a = jnp.concatenate([_rope_angles(t // GRID_W, half), _rope_angles(t % GRID_W, half)], axis=-1)
    lane = jnp.arange(HEAD_DIM)[None, :]
    low = (lane % half) < (half // 2)
    cos_a, sin_a = jnp.cos(ang_a), jnp.sin(ang_a)
    sa1 = jnp.where(low, -sin_a, 0.0)
    sa2 = jnp.where(low, 0.0, sin_a)
    ang_t = _rope_angles(t, HEAD_DIM)
    sin_t = jnp.where(lane < half, -jnp.sin(ang_t), jnp.sin(ang_t))
    return cos_a, sa1, sa2, jnp.cos(ang_t), sin_t


def _trunk(x, p):
    B, S, D = x.shape
    T = B * S
    qa_w = p["wa"].shape[0]
    qb_w = p["wb"].shape[0]
    n_cols = p["w_in"].shape[1]
    kv_w = (n_cols - 2 * D - qa_w - qb_w) // 4
    n_kv = kv_w // HEAD_DIM
    group = qa_w // kv_w
    assert qb_w == qa_w
    seg = (("qa", qa_w), ("ka", kv_w), ("va", kv_w), ("qb", qb_w), ("kb", kv_w), ("vb", kv_w),
           ("gate", 2 * D))
    col = {}
    off = 0
    for kind, n in seg:
        col[kind] = off
        off += n

    x2 = x.reshape(T, D)
    proj = _in_proj(x2, p["g_pre_mix"], p["w_in"], p["gq"], p["gk"], _rope_tables(S), S, seg)
    oa, ob = _attention(proj, p["sink"], B, S, n_kv, group, col)
    x1, h2 = _merge_out(oa, ob, proj, x2, p["wa"], p["wb"], p["wo"], p["g_post_mix"], p["g_pre_ffn"],
                        col["gate"])
    g = _ffn_up(h2, p["w_up"], p["conv_w"], p["conv_b"], S)
    y = _ffn_down(g, p["w_down"], x1, p["g_post_ffn"])
    return y.reshape(B, S, D)


def kernel(x_prompt, x_sample, norm_pre_mix, w_in, q_norm_a, k_norm_a, sink_b, w_branch_a, w_branch_b,
           w_out, norm_post_mix, norm_pre_ffn, w_up, conv_w, conv_b, w_down, norm_post_ffn):
    depth = w_in.shape[0]
    for l in range(depth):
        p = dict(
            g_pre_mix=norm_pre_mix[l][None, :], w_in=w_in[l].astype(BF16),
            gq=(q_norm_a[l] * Q_SCALE)[None, :], gk=k_norm_a[l][None, :],
            sink=sink_b[l], wa=w_branch_a[l].astype(BF16), wb=w_branch_b[l].astype(BF16),
            wo=w_out[l].astype(BF16), g_post_mix=norm_post_mix[l][None, :],
            g_pre_ffn=norm_pre_ffn[l][None, :], w_up=w_up[l].astype(BF16), conv_w=conv_w[l],
            conv_b=conv_b[l][None, :], w_down=w_down[l].astype(BF16),
            g_post_ffn=norm_post_ffn[l][None, :])
        x_prompt = _trunk(x_prompt, p)
        x_sample = _trunk(x_sample, p)
    return (x_prompt, x_sample)
```

```python
import functools
import math

import jax
import jax.numpy as jnp
from jax import lax
from jax.experimental import pallas as pl
from jax.experimental.pallas import tpu as pltpu

HEAD_DIM = 128
WINDOW = 128
GRID_W = 64
ROPE_THETA = 10000.0
EPS = 1e-6
LOG2E = 1.4426950408889634
Q_SCALE = HEAD_DIM ** -0.5 * LOG2E
LANES = 128
BF16_SUBLANES = 16
VMEM_LIMIT = 56 * 1024 * 1024

F32 = jnp.float32
BF16 = jnp.bfloat16


def _pick(n, candidates):
    for c in candidates:
        if n % c == 0:
            return c
    raise ValueError(f"no tile in {candidates} divides {n}")


def _rms(x, gain):
    r = lax.rsqrt(jnp.mean(x * x, axis=-1, keepdims=True) + EPS)
    return x * r * gain


def _gelu_tanh(x):
    k = math.sqrt(2.0 / math.pi)
    half = 0.5 * x
    return half + half * jnp.tanh(x * (k + (k * 0.044715) * (x * x)))


def _params(sem):
    return pltpu.CompilerParams(dimension_semantics=sem, vmem_limit_bytes=VMEM_LIMIT)


def _bufs(count, shape, dtype):
    return [pltpu.VMEM(shape, dtype) for _ in range(count)]


PIECE_BUFS = 3


def _in_proj_kernel(chunk_kind, piece, x_ref, g_ref, w_ref, gq_ref, gk_ref, cosa_ref, sa1_ref, sa2_ref,
                    cost_ref, sint_ref, o_ref, hs_ref, acc_bufs):
    hs_ref[...] = _rms(x_ref[...], g_ref[...]).astype(BF16)
    n_pieces = o_ref.shape[1] // piece

    half, quarter = HEAD_DIM // 2, HEAD_DIM // 4

    def axial(h):
        return (h * cosa_ref[...] + pltpu.roll(h, HEAD_DIM - quarter, 1) * sa1_ref[...]
                + pltpu.roll(h, quarter, 1) * sa2_ref[...])

    def rope1d(h):
        return h * cost_ref[...] + pltpu.roll(h, half, 1) * sint_ref[...]

    def epilogue(kind, h):
        if kind == "qa":
            return axial(_rms(h, gq_ref[...]))
        if kind == "ka":
            return axial(_rms(h, gk_ref[...]))
        if kind == "qb":
            return rope1d(h) * Q_SCALE
        if kind == "kb":
            return rope1d(h)
        if kind == "gate":
            return 0.5 + 0.5 * jnp.tanh(0.5 * h)
        return h

    for t in range(n_pieces + 2):
        if t < n_pieces:
            acc_bufs[t % PIECE_BUFS][...] = jnp.dot(hs_ref[...], w_ref[:, t * piece:(t + 1) * piece],
                                                    preferred_element_type=F32)
        k = t - 2
        if k >= 0:
            acc_ref = acc_bufs[k % PIECE_BUFS]
            for c in range(piece // LANES):
                col = k * piece + c * LANES
                h = acc_ref[:, c * LANES:(c + 1) * LANES]
                o_ref[:, col:col + LANES] = epilogue(chunk_kind[col // LANES], h).astype(o_ref.dtype)


def _in_proj(x, gain, w, gq, gk, tabs, seq, seg_kinds):
    T, D = x.shape
    N = w.shape[1]
    tm = _pick(seq, (256, 128))
    piece = _pick(N, (512, 256, 128))
    chunk_kind = []
    for kind, n in seg_kinds:
        chunk_kind += [kind] * (n // LANES)
    seq_tiles = seq // tm
    tab_spec = pl.BlockSpec((tm, LANES), lambda i: (i % seq_tiles, 0))
    vec = lambda n: pl.BlockSpec((1, n), lambda i: (0, 0))
    return pl.pallas_call(
        functools.partial(_in_proj_kernel, tuple(chunk_kind), piece),
        grid=(T // tm,),
        in_specs=[pl.BlockSpec((tm, D), lambda i: (i, 0)), vec(D), _resident(w.shape), vec(LANES), vec(LANES),
                  tab_spec, tab_spec, tab_spec, tab_spec, tab_spec],
        out_specs=pl.BlockSpec((tm, N), lambda i: (i, 0)),
        out_shape=jax.ShapeDtypeStruct((T, N), BF16),
        scratch_shapes=[pltpu.VMEM((tm, D), BF16), _bufs(PIECE_BUFS, (tm, piece), F32)],
        compiler_params=_params(("parallel",)),
        name="in_proj",
    )(x, gain, w, gq, gk, *tabs)


def _unstack_heads(o_ref, o, group, tq):
    for g in range(group):
        o_ref[:, g * HEAD_DIM:(g + 1) * HEAD_DIM] = o[g * tq:(g + 1) * tq].astype(o_ref.dtype)


def _qk(q, k):
    return lax.dot_general(q, k, (((1,), (1,)), ((), ())), preferred_element_type=F32)


ROW_BLOCK = 64
DEPTH = 3


def _attn_kernel(group, tq, tk, span, sink_ref, bias_ref, qa_ref, ka_ref, va_ref, qb_ref, kb_ref, vb_ref,
                 oa_ref, ob_ref, s_bufs, p_bufs, alpha_bufs, m_ref, acc_ref, qs_ref, qw_ref, wden_ref):
    h = pl.program_id(1)
    qi = pl.program_id(2)
    seq = ka_ref.shape[0]
    rows = group * tq
    n = seq // tk
    for g in range(group):
        qs_ref[g * tq:(g + 1) * tq, :] = qa_ref[:, g * HEAD_DIM:(g + 1) * HEAD_DIM]
        qw_ref[g * tq:(g + 1) * tq, :] = qb_ref[:, g * HEAD_DIM:(g + 1) * HEAD_DIM]
    m_ref[...] = jnp.full(m_ref.shape, -jnp.inf, F32)
    acc_ref[...] = jnp.zeros(acc_ref.shape, F32)
    row_blocks = [slice(r, r + ROW_BLOCK) for r in range(0, rows, ROW_BLOCK)]

    def with_ones(v):
        return jnp.concatenate([v, jnp.ones_like(v)], axis=1)

    def exp_block(s_ref, p_ref, sl, m, width):
        p_ref[sl, :width] = jnp.concatenate(
            [jnp.exp2(s_ref[sl, j * LANES:(j + 1) * LANES] - m) for j in range(width // LANES)],
            axis=1).astype(BF16)

    start = pl.multiple_of(jnp.clip(qi * tq - WINDOW, 0, seq - span), LANES)

    def win_scores(buf):
        s_bufs[buf][:, :span] = _qk(qw_ref[...], kb_ref[pl.ds(start, span), :])

    def win_softmax(buf):
        s_ref, p_ref, m_w = s_bufs[buf], p_bufs[buf], alpha_bufs[buf]
        for sl in row_blocks:
            t = sl.start % tq
            sink = sink_ref[h * group + sl.start // tq] * LOG2E
            sb = s_ref[sl, :span] + bias_ref[t:t + ROW_BLOCK, :]
            s_ref[sl, :span] = sb
            m = jnp.maximum(jnp.max(sb, axis=-1, keepdims=True), sink)
            m_w[sl, :] = jnp.broadcast_to(m, (ROW_BLOCK, LANES))
            wden_ref[sl, :] = jnp.broadcast_to(jnp.exp2(sink - m), (ROW_BLOCK, LANES))
        for sl in row_blocks:
            exp_block(s_ref, p_ref, sl, m_w[sl, :], span)

    def win_weighted(buf):
        pv = jnp.dot(p_bufs[buf][:, :span], with_ones(vb_ref[pl.ds(start, span), :]),
                     preferred_element_type=F32)
        _unstack_heads(ob_ref, pv[:, :HEAD_DIM] / (pv[:, HEAD_DIM:] + wden_ref[...]), group, tq)

    def chunk(ref, c):
        first = c * tk if isinstance(c, int) else pl.multiple_of(c * tk, tk)
        return ref[pl.ds(first, tk), :]

    def scores(c, buf):
        s_bufs[buf][:, :tk] = _qk(qs_ref[...], chunk(ka_ref, c))

    def softmax(buf):
        s_ref, p_ref, alpha_ref = s_bufs[buf], p_bufs[buf], alpha_bufs[buf]
        for sl in row_blocks:
            m_old = m_ref[sl, :]
            m_new = jnp.maximum(m_old, jnp.max(s_ref[sl, :tk], axis=-1, keepdims=True))
            alpha_ref[sl, :] = jnp.exp2(m_old - m_new)
            m_ref[sl, :] = m_new
        for sl in row_blocks:
            exp_block(s_ref, p_ref, sl, m_ref[sl, :], tk)

    def weighted(c, buf):
        pv = jnp.dot(p_bufs[buf][:, :tk], with_ones(chunk(va_ref, c)), preferred_element_type=F32)
        for sl in row_blocks:
            alpha = alpha_bufs[buf][sl, :]
            acc_ref[sl, :] = jnp.concatenate([alpha, alpha], axis=1) * acc_ref[sl, :] + pv[sl]

    n_items = n + 1

    def stage_scores(item, buf):
        if isinstance(item, int) and item == 0:
            win_scores(buf)
        else:
            scores(item - 1, buf)

    def stage_softmax(item, buf):
        if isinstance(item, int) and item == 0:
            win_softmax(buf)
        else:
            softmax(buf)

    def stage_weighted(item, buf):
        if isinstance(item, int) and item == 0:
            win_weighted(buf)
        else:
            weighted(item - 1, buf)

    def step(t, phase):
        static = isinstance(t, int)
        if not static or 0 <= t < n_items:
            stage_softmax(t, phase)
        if not static or 0 <= t + 1 < n_items:
            stage_scores(t + 1, (phase + 1) % DEPTH)
        if not static or 0 <= t - 2 < n_items:
            stage_weighted(t - 2, (phase - 2) % DEPTH)

    for t in range(-1, n_items + 2):
        step(t, t % DEPTH)
    acc = acc_ref[...]
    _unstack_heads(oa_ref, acc[:, :HEAD_DIM] / acc[:, HEAD_DIM:], group, tq)


def _band_bias(tq, span):
    r = jnp.arange(tq)[:, None]
    c = jnp.arange(span)[None, :]
    shifts = (0, -WINDOW, tq - span)
    return jnp.stack([jnp.where(jnp.abs(c + d - r) <= WINDOW, 0.0, -jnp.inf).astype(F32) for d in shifts])


def _attention(proj, sink, batch, seq, n_kv, group, col):
    T = proj.shape[0]
    tq = _pick(seq, (256, 128))
    tk = _pick(seq, (512, 256, 128))
    if seq // tk <= 4:
        tk = _pick(seq // 2, (1024, 512, 256, 128))
    span = min(tq + 2 * WINDOW, seq)
    assert (seq // tk) % 2 == 0
    width = max(tk, span)
    nq = seq // tq
    qw = group * HEAD_DIM
    rows = group * tq
    kind = lambda i: jnp.where(i == 0, 0, jnp.where(i == nq - 1, 2, 1))
    q_spec = lambda c: pl.BlockSpec((tq, qw), lambda b, h, i: (b * nq + i, c // qw + h))
    kv_spec = lambda c: pl.BlockSpec((seq, HEAD_DIM), lambda b, h, i: (b, c // HEAD_DIM + h))
    o_spec = pl.BlockSpec((tq, qw), lambda b, h, i: (b * nq + i, h))
    o_shape = jax.ShapeDtypeStruct((T, n_kv * qw), BF16)
    return pl.pallas_call(
        functools.partial(_attn_kernel, group, tq, tk, span),
        grid=(batch, n_kv, nq),
        in_specs=[pl.BlockSpec(memory_space=pltpu.SMEM),
                  pl.BlockSpec((None, tq, span), lambda b, h, i: (kind(i), 0, 0)),
                  q_spec(col["qa"]), kv_spec(col["ka"]), kv_spec(col["va"]),
                  q_spec(col["qb"]), kv_spec(col["kb"]), kv_spec(col["vb"])],
        out_specs=[o_spec, o_spec],
        out_shape=[o_shape, o_shape],
        scratch_shapes=[_bufs(DEPTH, (rows, width), F32), _bufs(DEPTH, (rows, width), BF16),
                        _bufs(DEPTH, (rows, LANES), F32),
                        pltpu.VMEM((rows, LANES), F32), pltpu.VMEM((rows, 2 * HEAD_DIM), F32),
                        pltpu.VMEM((rows, HEAD_DIM), BF16), pltpu.VMEM((rows, HEAD_DIM), BF16),
                        pltpu.VMEM((rows, LANES), F32)],
        compiler_params=_params(("parallel", "parallel", "arbitrary")),
        name="attention",
    )(sink, _band_bias(tq, span), proj, proj, proj, proj, proj, proj)


MERGE_SPLIT = 2


def _merge_kernel(n_gc, gw, *refs):
    oa_ref, ob_ref = refs[0], refs[1]
    ga_refs = refs[2:2 + n_gc]
    gb_refs = refs[2 + n_gc:2 + 2 * n_gc]
    x_ref, wa_ref, wb_ref, wo_ref, gpost_ref, gffn_ref, x1_ref, h2_ref = refs[2 + 2 * n_gc:]
    rows = x_ref.shape[0] // MERGE_SPLIT
    for r in range(0, x_ref.shape[0], rows):
        rs = slice(r, r + rows)
        oa = oa_ref[rs, :]
        ob = ob_ref[rs, :]
        parts = []
        for c in range(n_gc):
            sl = slice(c * gw, (c + 1) * gw)
            a = jnp.dot(oa, wa_ref[:, sl], preferred_element_type=F32)
            b = jnp.dot(ob, wb_ref[:, sl], preferred_element_type=F32)
            merged = ga_refs[c][rs, :].astype(F32) * a + gb_refs[c][rs, :].astype(F32) * b
            parts.append(merged.astype(BF16))
        merged = jnp.concatenate(parts, axis=1)
        mix = jnp.dot(merged, wo_ref[...], preferred_element_type=F32)
        x1 = x_ref[rs, :] + _rms(mix, gpost_ref[...])
        x1_ref[rs, :] = x1
        h2_ref[rs, :] = _rms(x1, gffn_ref[...]).astype(BF16)


def _resident(shape):
    return pl.BlockSpec(shape, lambda i: (0, 0), pipeline_mode=pl.Buffered(1))


def _merge_out(oa, ob, proj, x, wa, wb, wo, gpost, gffn, gate_col):
    T, D = x.shape
    qa_w, qb_w = oa.shape[1], ob.shape[1]
    tm = _pick(T, (512, 256, 128))
    gw = math.gcd(gate_col, D)
    n_gc = D // gw
    gate_specs = [pl.BlockSpec((tm, gw), functools.partial(lambda i, c: (i, c), c=gate_col // gw + c))
                  for c in range(2 * n_gc)]
    vec = pl.BlockSpec((1, D), lambda i: (0, 0))
    row = lambda n: pl.BlockSpec((tm, n), lambda i: (i, 0))
    return pl.pallas_call(
        functools.partial(_merge_kernel, n_gc, gw),
        grid=(T // tm,),
        in_specs=[row(qa_w), row(qb_w), *gate_specs, row(D),
                  _resident(wa.shape), _resident(wb.shape), _resident(wo.shape), vec, vec],
        out_specs=[row(D), row(D)],
        out_shape=[jax.ShapeDtypeStruct((T, D), F32), jax.ShapeDtypeStruct((T, D), BF16)],
        compiler_params=_params(("parallel",)),
        name="merge_out",
    )(oa, ob, *([proj] * (2 * n_gc)), x, wa, wb, wo, gpost, gffn)


def _ffn_up_kernel(tm, seq_tiles, hm_ref, hp_ref, hn_ref, wa_ref, wb_ref, cwa_ref, cwb_ref,
                   cba_ref, cbb_ref, o_ref, hs_ref):
    i = pl.program_id(0)
    j = pl.program_id(1)
    halo = BF16_SUBLANES

    @pl.when(j == 0)
    def _():
        first = (i % seq_tiles) == 0
        last = (i % seq_tiles) == seq_tiles - 1
        hp = hp_ref[...]
        hn = hn_ref[...]
        hs_ref[0:halo, :] = jnp.where(first, jnp.zeros_like(hp), hp)
        hs_ref[halo:halo + tm, :] = hm_ref[...]
        hs_ref[halo + tm:, :] = jnp.where(last, jnp.zeros_like(hn), hn)

    def conv(u, cw_ref, cb_ref):
        cw = cw_ref[...]
        c = cb_ref[...] + u[halo - 1:halo - 1 + tm] * cw[0:1]
        c = c + u[halo:halo + tm] * cw[1:2]
        return c + u[halo + 1:halo + 1 + tm] * cw[2:3]

    ua = jnp.dot(hs_ref[...], wa_ref[...], preferred_element_type=F32)
    ub = jnp.dot(hs_ref[...], wb_ref[...], preferred_element_type=F32)
    a = _gelu_tanh(conv(ua, cwa_ref, cba_ref))
    o_ref[...] = (a * conv(ub, cwb_ref, cbb_ref)).astype(o_ref.dtype)


def _ffn_up(h2, w_up, conv_w, conv_b, seq):
    T, D = h2.shape
    F = w_up.shape[1] // 2
    tm = _pick(seq, (1024, 512, 256, 128))
    tn = _pick(F, (512, 256, 128))
    nb = F // tn
    halo = BF16_SUBLANES
    per = tm // halo
    n_halo_blocks = T // halo
    seq_tiles = seq // tm
    return pl.pallas_call(
        functools.partial(_ffn_up_kernel, tm, seq_tiles),
        grid=(T // tm, nb),
        in_specs=[pl.BlockSpec((tm, D), lambda i, j: (i, 0)),
                  pl.BlockSpec((halo, D), lambda i, j: (jnp.maximum(i * per - 1, 0), 0)),
                  pl.BlockSpec((halo, D), lambda i, j: (jnp.minimum((i + 1) * per, n_halo_blocks - 1), 0)),
                  pl.BlockSpec((D, tn), lambda i, j: (0, j)),
                  pl.BlockSpec((D, tn), lambda i, j: (0, nb + j)),
                  pl.BlockSpec((3, tn), lambda i, j: (0, j)),
                  pl.BlockSpec((3, tn), lambda i, j: (0, nb + j)),
                  pl.BlockSpec((1, tn), lambda i, j: (0, j)),
                  pl.BlockSpec((1, tn), lambda i, j: (0, nb + j))],
        out_specs=pl.BlockSpec((tm, tn), lambda i, j: (i, j)),
        out_shape=jax.ShapeDtypeStruct((T, F), BF16),
        scratch_shapes=[pltpu.VMEM((tm + 2 * halo, D), BF16)],
        compiler_params=_params(("parallel", "arbitrary")),
        name="ffn_up",
    )(h2, h2, h2, w_up, w_up, conv_w, conv_w, conv_b, conv_b)


def _ffn_down_kernel(g_ref, w_ref, x1_ref, gain_ref, o_ref):
    f = jnp.dot(g_ref[...], w_ref[...], preferred_element_type=F32)
    o_ref[...] = x1_ref[...] + _rms(f, gain_ref[...])


def _ffn_down(g, w_down, x1, gain):
    T, F = g.shape
    D = x1.shape[1]
    tm = _pick(T, (256, 128))
    return pl.pallas_call(
        _ffn_down_kernel,
        grid=(T // tm,),
        in_specs=[pl.BlockSpec((tm, F), lambda i: (i, 0)), _resident(w_down.shape),
                  pl.BlockSpec((tm, D), lambda i: (i, 0)), pl.BlockSpec((1, D), lambda i: (0, 0))],
        out_specs=pl.BlockSpec((tm, D), lambda i: (i, 0)),
        out_shape=jax.ShapeDtypeStruct((T, D), F32),
        compiler_params=_params(("parallel",)),
        name="ffn_down",
    )(g, w_down, x1, gain)


def _rope_angles(pos, dim):
    inv_freq = ROPE_THETA ** (-jnp.arange(0, dim, 2, dtype=F32) / dim)
    ang = pos.astype(F32)[:, None] * inv_freq[None, :]
    return jnp.concatenate([ang, ang], axis=-1)


def _rope_tables(seq):
    t = jnp.arange(seq, dtype=jnp.int32)
    half = HEAD_DIM // 2
    ang_a = jnp.concatenate([_rope_angles(t // GRID_W, half), _rope_angles(t % GRID_W, half)], axis=-1)
    lane = jnp.arange(HEAD_DIM)[None, :]
    low = (lane % half) < (half // 2)
    cos_a, sin_a = jnp.cos(ang_a), jnp.sin(ang_a)
    sa1 = jnp.where(low, -sin_a, 0.0)
    sa2 = jnp.where(low, 0.0, sin_a)
    ang_t = _rope_angles(t, HEAD_DIM)
    sin_t = jnp.where(lane < half, -jnp.sin(ang_t), jnp.sin(ang_t))
    return cos_a, sa1, sa2, jnp.cos(ang_t), sin_t


def _trunk(x, p):
    B, S, D = x.shape
    T = B * S
    qa_w = p["wa"].shape[0]
    qb_w = p["wb"].shape[0]
    n_cols = p["w_in"].shape[1]
    kv_w = (n_cols - 2 * D - qa_w - qb_w) // 4
    n_kv = kv_w // HEAD_DIM
    group = qa_w // kv_w
    assert qb_w == qa_w
    seg = (("qa", qa_w), ("ka", kv_w), ("va", kv_w), ("qb", qb_w), ("kb", kv_w), ("vb", kv_w),
           ("gate", 2 * D))
    col = {}
    off = 0
    for kind, n in seg:
        col[kind] = off
        off += n

    x2 = x.reshape(T, D)
    proj = _in_proj(x2, p["g_pre_mix"], p["w_in"], p["gq"], p["gk"], _rope_tables(S), S, seg)
    oa, ob = _attention(proj, p["sink"], B, S, n_kv, group, col)
    x1, h2 = _merge_out(oa, ob, proj, x2, p["wa"], p["wb"], p["wo"], p["g_post_mix"], p["g_pre_ffn"],
                        col["gate"])
    g = _ffn_up(h2, p["w_up"], p["conv_w"], p["conv_b"], S)
    y = _ffn_down(g, p["w_down"], x1, p["g_post_ffn"])
    return y.reshape(B, S, D)


def kernel(x_prompt, x_sample, norm_pre_mix, w_in, q_norm_a, k_norm_a, sink_b, w_branch_a, w_branch_b,
           w_out, norm_post_mix, norm_pre_ffn, w_up, conv_w, conv_b, w_down, norm_post_ffn):
    depth = w_in.shape[0]
    for l in range(depth):
        p = dict(
            g_pre_mix=norm_pre_mix[l][None, :], w_in=w_in[l].astype(BF16),
            gq=(q_norm_a[l] * Q_SCALE)[None, :], gk=k_norm_a[l][None, :],
            sink=sink_b[l], wa=w_branch_a[l].astype(BF16), wb=w_branch_b[l].astype(BF16),
            wo=w_out[l].astype(BF16), g_post_mix=norm_post_mix[l][None, :],
            g_pre_ffn=norm_pre_ffn[l][None, :], w_up=w_up[l].astype(BF16), conv_w=conv_w[l],
            conv_b=conv_b[l][None, :], w_down=w_down[l].astype(BF16),
            g_post_ffn=norm_post_ffn[l][None, :])
        x_prompt = _trunk(x_prompt, p)
        x_sample = _trunk(x_sample, p)
    return (x_prompt, x_sample)
```

```python
import functools
import math

import jax
import jax.numpy as jnp
from jax import lax
from jax.experimental import pallas as pl
from jax.experimental.pallas import tpu as pltpu

HEAD_DIM = 128
WINDOW = 128
GRID_W = 64
ROPE_THETA = 10000.0
EPS = 1e-6
LOG2E = 1.4426950408889634
Q_SCALE = HEAD_DIM ** -0.5 * LOG2E
LANES = 128
BF16_SUBLANES = 16
VMEM_LIMIT = 56 * 1024 * 1024

F32 = jnp.float32
BF16 = jnp.bfloat16


def _pick(n, candidates):
    for c in candidates:
        if n % c == 0:
            return c
    raise ValueError(f"no tile in {candidates} divides {n}")


def _rms(x, gain):
    r = lax.rsqrt(jnp.mean(x * x, axis=-1, keepdims=True) + EPS)
    return x * r * gain


def _gelu_tanh(x):
    k = math.sqrt(2.0 / math.pi)
    half = 0.5 * x
    return half + half * jnp.tanh(x * (k + (k * 0.044715) * (x * x)))


def _params(sem):
    return pltpu.CompilerParams(dimension_semantics=sem, vmem_limit_bytes=VMEM_LIMIT)


def _bufs(count, shape, dtype):
    return [pltpu.VMEM(shape, dtype) for _ in range(count)]


PIECE_BUFS = 3


def _in_proj_kernel(chunk_kind, piece, x_ref, g_ref, w_ref, gq_ref, gk_ref, cosa_ref, sa1_ref, sa2_ref,
                    cost_ref, sint_ref, o_ref, hs_ref, acc_bufs):
    hs_ref[...] = _rms(x_ref[...], g_ref[...]).astype(BF16)
    n_pieces = o_ref.shape[1] // piece

    half, quarter = HEAD_DIM // 2, HEAD_DIM // 4

    def axial(h):
        return (h * cosa_ref[...] + pltpu.roll(h, HEAD_DIM - quarter, 1) * sa1_ref[...]
                + pltpu.roll(h, quarter, 1) * sa2_ref[...])

    def rope1d(h):
        return h * cost_ref[...] + pltpu.roll(h, half, 1) * sint_ref[...]

    def epilogue(kind, h):
        if kind == "qa":
            return axial(_rms(h, gq_ref[...]))
        if kind == "ka":
            return axial(_rms(h, gk_ref[...]))
        if kind == "qb":
            return rope1d(h) * Q_SCALE
        if kind == "kb":
            return rope1d(h)
        if kind == "gate":
            return 0.5 + 0.5 * jnp.tanh(0.5 * h)
        return h

    for t in range(n_pieces + 2):
        if t < n_pieces:
            acc_bufs[t % PIECE_BUFS][...] = jnp.dot(hs_ref[...], w_ref[:, t * piece:(t + 1) * piece],
                                                    preferred_element_type=F32)
        k = t - 2
        if k >= 0:
            acc_ref = acc_bufs[k % PIECE_BUFS]
            for c in range(piece // LANES):
                col = k * piece + c * LANES
                h = acc_ref[:, c * LANES:(c + 1) * LANES]
                o_ref[:, col:col + LANES] = epilogue(chunk_kind[col // LANES], h).astype(o_ref.dtype)


def _in_proj(x, gain, w, gq, gk, tabs, seq, seg_kinds):
    T, D = x.shape
    N = w.shape[1]
    tm = _pick(seq, (256, 128))
    piece = _pick(N, (512, 256, 128))
    chunk_kind = []
    for kind, n in seg_kinds:
        chunk_kind += [kind] * (n // LANES)
    seq_tiles = seq // tm
    tab_spec = pl.BlockSpec((tm, LANES), lambda i: (i % seq_tiles, 0))
    vec = lambda n: pl.BlockSpec((1, n), lambda i: (0, 0))
    return pl.pallas_call(
        functools.partial(_in_proj_kernel, tuple(chunk_kind), piece),
        grid=(T // tm,),
        in_specs=[pl.BlockSpec((tm, D), lambda i: (i, 0)), vec(D), _resident(w.shape), vec(LANES), vec(LANES),
                  tab_spec, tab_spec, tab_spec, tab_spec, tab_spec],
        out_specs=pl.BlockSpec((tm, N), lambda i: (i, 0)),
        out_shape=jax.ShapeDtypeStruct((T, N), BF16),
        scratch_shapes=[pltpu.VMEM((tm, D), BF16), _bufs(PIECE_BUFS, (tm, piece), F32)],
        compiler_params=_params(("parallel",)),
        name="in_proj",
    )(x, gain, w, gq, gk, *tabs)


def _unstack_heads(o_ref, o, group, tq):
    for g in range(group):
        o_ref[:, g * HEAD_DIM:(g + 1) * HEAD_DIM] = o[g * tq:(g + 1) * tq].astype(o_ref.dtype)


def _qk(q, k):
    return lax.dot_general(q, k, (((1,), (1,)), ((), ())), preferred_element_type=F32)


ROW_BLOCK = 64
DEPTH = 3


def _attn_kernel(group, tq, tk, span, sink_ref, bias_ref, qa_ref, ka_ref, va_ref, qb_ref, kb_ref, vb_ref,
                 oa_ref, ob_ref, s_bufs, p_bufs, alpha_bufs, m_ref, acc_ref, qs_ref, qw_ref, wden_ref):
    h = pl.program_id(1)
    qi = pl.program_id(2)
    seq = ka_ref.shape[0]
    rows = group * tq
    n = seq // tk
    for g in range(group):
        qs_ref[g * tq:(g + 1) * tq, :] = qa_ref[:, g * HEAD_DIM:(g + 1) * HEAD_DIM]
        qw_ref[g * tq:(g + 1) * tq, :] = qb_ref[:, g * HEAD_DIM:(g + 1) * HEAD_DIM]
    m_ref[...] = jnp.full(m_ref.shape, -jnp.inf, F32)
    acc_ref[...] = jnp.zeros(acc_ref.shape, F32)
    row_blocks = [slice(r, r + ROW_BLOCK) for r in range(0, rows, ROW_BLOCK)]

    def with_ones(v):
        return jnp.concatenate([v, jnp.ones_like(v)], axis=1)

    def exp_block(s_ref, p_ref, sl, m, width):
        p_ref[sl, :width] = jnp.concatenate(
            [jnp.exp2(s_ref[sl, j * LANES:(j + 1) * LANES] - m) for j in range(width // LANES)],
            axis=1).astype(BF16)

    start = pl.multiple_of(jnp.clip(qi * tq - WINDOW, 0, seq - span), LANES)

    def win_scores(buf):
        s_bufs[buf][:, :span] = _qk(qw_ref[...], kb_ref[pl.ds(start, span), :])

    def win_softmax(buf):
        s_ref, p_ref, m_w = s_bufs[buf], p_bufs[buf], alpha_bufs[buf]
        for sl in row_blocks:
            t = sl.start % tq
            sink = sink_ref[h * group + sl.start // tq] * LOG2E
            sb = s_ref[sl, :span] + bias_ref[t:t + ROW_BLOCK, :]
            s_ref[sl, :span] = sb
            m = jnp.maximum(jnp.max(sb, axis=-1, keepdims=True), sink)
            m_w[sl, :] = jnp.broadcast_to(m, (ROW_BLOCK, LANES))
            wden_ref[sl, :] = jnp.broadcast_to(jnp.exp2(sink - m), (ROW_BLOCK, LANES))
        for sl in row_blocks:
            exp_block(s_ref, p_ref, sl, m_w[sl, :], span)

    def win_weighted(buf):
        pv = jnp.dot(p_bufs[buf][:, :span], with_ones(vb_ref[pl.ds(start, span), :]),
                     preferred_element_type=F32)
        _unstack_heads(ob_ref, pv[:, :HEAD_DIM] / (pv[:, HEAD_DIM:] + wden_ref[...]), group, tq)

    def chunk(ref, c):
        first = c * tk if isinstance(c, int) else pl.multiple_of(c * tk, tk)
        return ref[pl.ds(first, tk), :]

    def scores(c, buf):
        s_bufs[buf][:, :tk] = _qk(qs_ref[...], chunk(ka_ref, c))

    def softmax(buf):
        s_ref, p_ref, alpha_ref = s_bufs[buf], p_bufs[buf], alpha_bufs[buf]
        for sl in row_blocks:
            m_old = m_ref[sl, :]
            m_new = jnp.maximum(m_old, jnp.max(s_ref[sl, :tk], axis=-1, keepdims=True))
            alpha_ref[sl, :] = jnp.exp2(m_old - m_new)
            m_ref[sl, :] = m_new
        for sl in row_blocks:
            exp_block(s_ref, p_ref, sl, m_ref[sl, :], tk)

    def weighted(c, buf):
        pv = jnp.dot(p_bufs[buf][:, :tk], with_ones(chunk(va_ref, c)), preferred_element_type=F32)
        for sl in row_blocks:
            alpha = alpha_bufs[buf][sl, :]
            acc_ref[sl, :] = jnp.concatenate([alpha, alpha], axis=1) * acc_ref[sl, :] + pv[sl]

    n_items = n + 1

    def stage_scores(item, buf):
        if isinstance(item, int) and item == 0:
            win_scores(buf)
        else:
            scores(item - 1, buf)

    def stage_softmax(item, buf):
        if isinstance(item, int) and item == 0:
            win_softmax(buf)
        else:
            softmax(buf)

    def stage_weighted(item, buf):
        if isinstance(item, int) and item == 0:
            win_weighted(buf)
        else:
            weighted(item - 1, buf)

    def step(t, phase):
        static = isinstance(t, int)
        if not static or 0 <= t < n_items:
            stage_softmax(t, phase)
        if not static or 0 <= t + 1 < n_items:
            stage_scores(t + 1, (phase + 1) % DEPTH)
        if not static or 0 <= t - 2 < n_items:
            stage_weighted(t - 2, (phase - 2) % DEPTH)

    for t in range(-1, n_items + 2):
        step(t, t % DEPTH)
    acc = acc_ref[...]
    _unstack_heads(oa_ref, acc[:, :HEAD_DIM] / acc[:, HEAD_DIM:], group, tq)


def _band_bias(tq, span):
    r = jnp.arange(tq)[:, None]
    c = jnp.arange(span)[None, :]
    shifts = (0, -WINDOW, tq - span)
    return jnp.stack([jnp.where(jnp.abs(c + d - r) <= WINDOW, 0.0, -jnp.inf).astype(F32) for d in shifts])


def _attention(proj, sink, batch, seq, n_kv, group, col):
    T = proj.shape[0]
    tq = _pick(seq, (256, 128))
    tk = _pick(seq, (512, 256, 128))
    if seq // tk <= 4:
        tk = _pick(seq // 2, (1024, 512, 256, 128))
    span = min(tq + 2 * WINDOW, seq)
    assert (seq // tk) % 2 == 0
    width = max(tk, span)
    nq = seq // tq
    qw = group * HEAD_DIM
    rows = group * tq
    kind = lambda i: jnp.where(i == 0, 0, jnp.where(i == nq - 1, 2, 1))
    q_spec = lambda c: pl.BlockSpec((tq, qw), lambda b, h, i: (b * nq + i, c // qw + h))
    kv_spec = lambda c: pl.BlockSpec((seq, HEAD_DIM), lambda b, h, i: (b, c // HEAD_DIM + h))
    o_spec = pl.BlockSpec((tq, qw), lambda b, h, i: (b * nq + i, h))
    o_shape = jax.ShapeDtypeStruct((T, n_kv * qw), BF16)
    return pl.pallas_call(
        functools.partial(_attn_kernel, group, tq, tk, span),
        grid=(batch, n_kv, nq),
        in_specs=[pl.BlockSpec(memory_space=pltpu.SMEM),
                  pl.BlockSpec((None, tq, span), lambda b, h, i: (kind(i), 0, 0)),
                  q_spec(col["qa"]), kv_spec(col["ka"]), kv_spec(col["va"]),
                  q_spec(col["qb"]), kv_spec(col["kb"]), kv_spec(col["vb"])],
        out_specs=[o_spec, o_spec],
        out_shape=[o_shape, o_shape],
        scratch_shapes=[_bufs(DEPTH, (rows, width), F32), _bufs(DEPTH, (rows, width), BF16),
                        _bufs(DEPTH, (rows, LANES), F32),
                        pltpu.VMEM((rows, LANES), F32), pltpu.VMEM((rows, 2 * HEAD_DIM), F32),
                        pltpu.VMEM((rows, HEAD_DIM), BF16), pltpu.VMEM((rows, HEAD_DIM), BF16),
                        pltpu.VMEM((rows, LANES), F32)],
        compiler_params=_params(("parallel", "parallel", "arbitrary")),
        name="attention",
    )(sink, _band_bias(tq, span), proj, proj, proj, proj, proj, proj)


MERGE_SPLIT = 2


def _merge_kernel(n_gc, gw, *refs):
    oa_ref, ob_ref = refs[0], refs[1]
    ga_refs = refs[2:2 + n_gc]
    gb_refs = refs[2 + n_gc:2 + 2 * n_gc]
    x_ref, wa_ref, wb_ref, wo_ref, gpost_ref, gffn_ref, x1_ref, h2_ref = refs[2 + 2 * n_gc:]
    rows = x_ref.shape[0] // MERGE_SPLIT
    for r in range(0, x_ref.shape[0], rows):
        rs = slice(r, r + rows)
        oa = oa_ref[rs, :]
        ob = ob_ref[rs, :]
        parts = []
        for c in range(n_gc):
            sl = slice(c * gw, (c + 1) * gw)
            a = jnp.dot(oa, wa_ref[:, sl], preferred_element_type=F32)
            b = jnp.dot(ob, wb_ref[:, sl], preferred_element_type=F32)
            merged = ga_refs[c][rs, :].astype(F32) * a + gb_refs[c][rs, :].astype(F32) * b
            parts.append(merged.astype(BF16))
        merged = jnp.concatenate(parts, axis=1)
        mix = jnp.dot(merged, wo_ref[...], preferred_element_type=F32)
        x1 = x_ref[rs, :] + _rms(mix, gpost_ref[...])
        x1_ref[rs, :] = x1
        h2_ref[rs, :] = _rms(x1, gffn_ref[...]).astype(BF16)


def _resident(shape):
    return pl.BlockSpec(shape, lambda i: (0, 0), pipeline_mode=pl.Buffered(1))


def _merge_out(oa, ob, proj, x, wa, wb, wo, gpost, gffn, gate_col):
    T, D = x.shape
    qa_w, qb_w = oa.shape[1], ob.shape[1]
    tm = _pick(T, (512, 256, 128))
    gw = math.gcd(gate_col, D)
    n_gc = D // gw
    gate_specs = [pl.BlockSpec((tm, gw), functools.partial(lambda i, c: (i, c), c=gate_col // gw + c))
                  for c in range(2 * n_gc)]
    vec = pl.BlockSpec((1, D), lambda i: (0, 0))
    row = lambda n: pl.BlockSpec((tm, n), lambda i: (i, 0))
    return pl.pallas_call(
        functools.partial(_merge_kernel, n_gc, gw),
        grid=(T // tm,),
        in_specs=[row(qa_w), row(qb_w), *gate_specs, row(D),
                  _resident(wa.shape), _resident(wb.shape), _resident(wo.shape), vec, vec],
        out_specs=[row(D), row(D)],
        out_shape=[jax.ShapeDtypeStruct((T, D), F32), jax.ShapeDtypeStruct((T, D), BF16)],
        compiler_params=_params(("parallel",)),
        name="merge_out",
    )(oa, ob, *([proj] * (2 * n_gc)), x, wa, wb, wo, gpost, gffn)


def _ffn_up_kernel(tm, seq_tiles, hm_ref, hp_ref, hn_ref, wa_ref, wb_ref, cwa_ref, cwb_ref,
                   cba_ref, cbb_ref, o_ref, hs_ref):
    i = pl.program_id(0)
    j = pl.program_id(1)
    halo = BF16_SUBLANES

    @pl.when(j == 0)
    def _():
        first = (i % seq_tiles) == 0
        last = (i % seq_tiles) == seq_tiles - 1
        hp = hp_ref[...]
        hn = hn_ref[...]
        hs_ref[0:halo, :] = jnp.where(first, jnp.zeros_like(hp), hp)
        hs_ref[halo:halo + tm, :] = hm_ref[...]
        hs_ref[halo + tm:, :] = jnp.where(last, jnp.zeros_like(hn), hn)

    def conv(u, cw_ref, cb_ref):
        cw = cw_ref[...]
        c = cb_ref[...] + u[halo - 1:halo - 1 + tm] * cw[0:1]
        c = c + u[halo:halo + tm] * cw[1:2]
        return c + u[halo + 1:halo + 1 + tm] * cw[2:3]

    ua = jnp.dot(hs_ref[...], wa_ref[...], preferred_element_type=F32)
    ub = jnp.dot(hs_ref[...], wb_ref[...], preferred_element_type=F32)
    a = _gelu_tanh(conv(ua, cwa_ref, cba_ref))
    o_ref[...] = (a * conv(ub, cwb_ref, cbb_ref)).astype(o_ref.dtype)


def _ffn_up(h2, w_up, conv_w, conv_b, seq):
    T, D = h2.shape
    F = w_up.shape[1] // 2
    tm = _pick(seq, (1024, 512, 256, 128))
    tn = _pick(F, (512, 256, 128))
    nb = F // tn
    halo = BF16_SUBLANES
    per = tm // halo
    n_halo_blocks = T // halo
    seq_tiles = seq // tm
    return pl.pallas_call(
        functools.partial(_ffn_up_kernel, tm, seq_tiles),
        grid=(T // tm, nb),
        in_specs=[pl.BlockSpec((tm, D), lambda i, j: (i, 0)),
                  pl.BlockSpec((halo, D), lambda i, j: (jnp.maximum(i * per - 1, 0), 0)),
                  pl.BlockSpec((halo, D), lambda i, j: (jnp.minimum((i + 1) * per, n_halo_blocks - 1), 0)),
                  pl.BlockSpec((D, tn), lambda i, j: (0, j)),
                  pl.BlockSpec((D, tn), lambda i, j: (0, nb + j)),
                  pl.BlockSpec((3, tn), lambda i, j: (0, j)),
                  pl.BlockSpec((3, tn), lambda i, j: (0, nb + j)),
                  pl.BlockSpec((1, tn), lambda i, j: (0, j)),
                  pl.BlockSpec((1, tn), lambda i, j: (0, nb + j))],
        out_specs=pl.BlockSpec((tm, tn), lambda i, j: (i, j)),
        out_shape=jax.ShapeDtypeStruct((T, F), BF16),
        scratch_shapes=[pltpu.VMEM((tm + 2 * halo, D), BF16)],
        compiler_params=_params(("parallel", "arbitrary")),
        name="ffn_up",
    )(h2, h2, h2, w_up, w_up, conv_w, conv_w, conv_b, conv_b)


def _ffn_down_kernel(g_ref, w_ref, x1_ref, gain_ref, o_ref):
    rows = o_ref.shape[0] // MERGE_SPLIT
    for r in range(0, o_ref.shape[0], rows):
        rs = slice(r, r + rows)
        f = jnp.dot(g_ref[rs, :], w_ref[...], preferred_element_type=F32)
        o_ref[rs, :] = x1_ref[rs, :] + _rms(f, gain_ref[...])


def _ffn_down(g, w_down, x1, gain):
    T, F = g.shape
    D = x1.shape[1]
    tm = _pick(T, (512, 256, 128))
    return pl.pallas_call(
        _ffn_down_kernel,
        grid=(T // tm,),
        in_specs=[pl.BlockSpec((tm, F), lambda i: (i, 0)), _resident(w_down.shape),
                  pl.BlockSpec((tm, D), lambda i: (i, 0)), pl.BlockSpec((1, D), lambda i: (0, 0))],
        out_specs=pl.BlockSpec((tm, D), lambda i: (i, 0)),
        out_shape=jax.ShapeDtypeStruct((T, D), F32),
        compiler_params=_params(("parallel",)),
        name="ffn_down",
    )(g, w_down, x1, gain)


def _rope_angles(pos, dim):
    inv_freq = ROPE_THETA ** (-jnp.arange(0, dim, 2, dtype=F32) / dim)
    ang = pos.astype(F32)[:, None] * inv_freq[None, :]
    return jnp.concatenate([ang, ang], axis=-1)


def _rope_tables(seq):
    t = jnp.arange(seq, dtype=jnp.int32)
    half = HEAD_DIM // 2
    ang_a = jnp.concatenate([_rope_angles(t // GRID_W, half), _rope_angles(t % GRID_W, half)], axis=-1)
    lane = jnp.arange(HEAD_DIM)[None, :]
    low = (lane % half) < (half // 2)
    cos_a, sin_a = jnp.cos(ang_a), jnp.sin(ang_a)
    sa1 = jnp.where(low, -sin_a, 0.0)
    sa2 = jnp.where(low, 0.0, sin_a)
    ang_t = _rope_angles(t, HEAD_DIM)
    sin_t = jnp.where(lane < half, -jnp.sin(ang_t), jnp.sin(ang_t))
    return cos_a, sa1, sa2, jnp.cos(ang_t), sin_t


def _trunk(x, p):
    B, S, D = x.shape
    T = B * S
    qa_w = p["wa"].shape[0]
    qb_w = p["wb"].shape[0]
    n_cols = p["w_in"].shape[1]
    kv_w = (n_cols - 2 * D - qa_w - qb_w) // 4
    n_kv = kv_w // HEAD_DIM
    group = qa_w // kv_w
    assert qb_w == qa_w
    seg = (("qa", qa_w), ("ka", kv_w), ("va", kv_w), ("qb", qb_w), ("kb", kv_w), ("vb", kv_w),
           ("gate", 2 * D))
    col = {}
    off = 0
    for kind, n in seg:
        col[kind] = off
        off += n

    x2 = x.reshape(T, D)
    proj = _in_proj(x2, p["g_pre_mix"], p["w_in"], p["gq"], p["gk"], _rope_tables(S), S, seg)
    oa, ob = _attention(proj, p["sink"], B, S, n_kv, group, col)
    x1, h2 = _merge_out(oa, ob, proj, x2, p["wa"], p["wb"], p["wo"], p["g_post_mix"], p["g_pre_ffn"],
                        col["gate"])
    g = _ffn_up(h2, p["w_up"], p["conv_w"], p["conv_b"], S)
    y = _ffn_down(g, p["w_down"], x1, p["g_post_ffn"])
    return y.reshape(B, S, D)


def kernel(x_prompt, x_sample, norm_pre_mix, w_in, q_norm_a, k_norm_a, sink_b, w_branch_a, w_branch_b,
           w_out, norm_post_mix, norm_pre_ffn, w_up, conv_w, conv_b, w_down, norm_post_ffn):
    depth = w_in.shape[0]
    for l in range(depth):
        p = dict(
            g_pre_mix=norm_pre_mix[l][None, :], w_in=w_in[l].astype(BF16),
            gq=(q_norm_a[l] * Q_SCALE)[None, :], gk=k_norm_a[l][None, :],
            sink=sink_b[l], wa=w_branch_a[l].astype(BF16), wb=w_branch_b[l].astype(BF16),
            wo=w_out[l].astype(BF16), g_post_mix=norm_post_mix[l][None, :],
            g_pre_ffn=norm_pre_ffn[l][None, :], w_up=w_up[l].astype(BF16), conv_w=conv_w[l],
            conv_b=conv_b[l][None, :], w_down=w_down[l].astype(BF16),
            g_post_ffn=norm_post_ffn[l][None, :])
        x_prompt = _trunk(x_prompt, p)
        x_sample = _trunk(x_sample, p)
    return (x_prompt, x_sample)
```

```python
import functools
import math

import jax
import jax.numpy as jnp
from jax import lax
from jax.experimental import pallas as pl
from jax.experimental.pallas import tpu as pltpu

HEAD_DIM = 128
WINDOW = 128
GRID_W = 64
ROPE_THETA = 10000.0
EPS = 1e-6
LOG2E = 1.4426950408889634
Q_SCALE = HEAD_DIM ** -0.5 * LOG2E
LANES = 128
BF16_SUBLANES = 16
VMEM_LIMIT = 56 * 1024 * 1024

F32 = jnp.float32
BF16 = jnp.bfloat16


def _pick(n, candidates):
    for c in candidates:
        if n % c == 0:
            return c
    raise ValueError(f"no tile in {candidates} divides {n}")


def _rms(x, gain):
    r = lax.rsqrt(jnp.mean(x * x, axis=-1, keepdims=True) + EPS)
    return x * r * gain


def _gelu_tanh(x):
    k = math.sqrt(2.0 / math.pi)
    half = 0.5 * x
    return half + half * jnp.tanh(x * (k + (k * 0.044715) * (x * x)))


def _params(sem):
    return pltpu.CompilerParams(dimension_semantics=sem, vmem_limit_bytes=VMEM_LIMIT)


def _bufs(count, shape, dtype):
    return [pltpu.VMEM(shape, dtype) for _ in range(count)]


PIECE_BUFS = 3


def _in_proj_kernel(chunk_kind, piece, x_ref, g_ref, w_ref, gq_ref, gk_ref, cosa_ref, sa1_ref, sa2_ref,
                    cost_ref, sint_ref, o_ref, hs_ref, acc_bufs):
    hs_ref[...] = _rms(x_ref[...], g_ref[...]).astype(BF16)
    n_pieces = o_ref.shape[1] // piece

    half, quarter = HEAD_DIM // 2, HEAD_DIM // 4

    def axial(h):
        return (h * cosa_ref[...] + pltpu.roll(h, HEAD_DIM - quarter, 1) * sa1_ref[...]
                + pltpu.roll(h, quarter, 1) * sa2_ref[...])

    def rope1d(h):
        return h * cost_ref[...] + pltpu.roll(h, half, 1) * sint_ref[...]

    def epilogue(kind, h):
        if kind == "qa":
            return axial(_rms(h, gq_ref[...]))
        if kind == "ka":
            return axial(_rms(h, gk_ref[...]))
        if kind == "qb":
            return rope1d(h) * Q_SCALE
        if kind == "kb":
            return rope1d(h)
        if kind == "gate":
            return 0.5 + 0.5 * jnp.tanh(0.5 * h)
        return h

    for t in range(n_pieces + 2):
        if t < n_pieces:
            acc_bufs[t % PIECE_BUFS][...] = jnp.dot(hs_ref[...], w_ref[:, t * piece:(t + 1) * piece],
                                                    preferred_element_type=F32)
        k = t - 2
        if k >= 0:
            acc_ref = acc_bufs[k % PIECE_BUFS]
            for c in range(piece // LANES):
                col = k * piece + c * LANES
                h = acc_ref[:, c * LANES:(c + 1) * LANES]
                o_ref[:, col:col + LANES] = epilogue(chunk_kind[col // LANES], h).astype(o_ref.dtype)


def _in_proj(x, gain, w, gq, gk, tabs, seq, seg_kinds):
    T, D = x.shape
    N = w.shape[1]
    tm = _pick(seq, (256, 128))
    piece = _pick(N, (512, 256, 128))
    chunk_kind = []
    for kind, n in seg_kinds:
        chunk_kind += [kind] * (n // LANES)
    seq_tiles = seq // tm
    tab_spec = pl.BlockSpec((tm, LANES), lambda i: (i % seq_tiles, 0))
    vec = lambda n: pl.BlockSpec((1, n), lambda i: (0, 0))
    return pl.pallas_call(
        functools.partial(_in_proj_kernel, tuple(chunk_kind), piece),
        grid=(T // tm,),
        in_specs=[pl.BlockSpec((tm, D), lambda i: (i, 0)), vec(D), _resident(w.shape), vec(LANES), vec(LANES),
                  tab_spec, tab_spec, tab_spec, tab_spec, tab_spec],
        out_specs=pl.BlockSpec((tm, N), lambda i: (i, 0)),
        out_shape=jax.ShapeDtypeStruct((T, N), BF16),
        scratch_shapes=[pltpu.VMEM((tm, D), BF16), _bufs(PIECE_BUFS, (tm, piece), F32)],
        compiler_params=_params(("parallel",)),
        name="in_proj",
    )(x, gain, w, gq, gk, *tabs)


def _unstack_heads(o_ref, o, group, tq):
    for g in range(group):
        o_ref[:, g * HEAD_DIM:(g + 1) * HEAD_DIM] = o[g * tq:(g + 1) * tq].astype(o_ref.dtype)


def _qk(q, k):
    return lax.dot_general(q, k, (((1,), (1,)), ((), ())), preferred_element_type=F32)


ROW_BLOCK = 64
DEPTH = 3


def _attn_kernel(group, tq, tk, span, sink_ref, bias_ref, qa_ref, ka_ref, va_ref, qb_ref, kb_ref, vb_ref,
                 oa_ref, ob_ref, s_bufs, p_bufs, alpha_bufs, m_ref, acc_ref, qs_ref, qw_ref, wden_ref):
    h = pl.program_id(1)
    qi = pl.program_id(2)
    seq = ka_ref.shape[0]
    rows = group * tq
    n = seq // tk
    for g in range(group):
        qs_ref[g * tq:(g + 1) * tq, :] = qa_ref[:, g * HEAD_DIM:(g + 1) * HEAD_DIM]
        qw_ref[g * tq:(g + 1) * tq, :] = qb_ref[:, g * HEAD_DIM:(g + 1) * HEAD_DIM]
    m_ref[...] = jnp.full(m_ref.shape, -jnp.inf, F32)
    acc_ref[...] = jnp.zeros(acc_ref.shape, F32)
    row_blocks = [slice(r, r + ROW_BLOCK) for r in range(0, rows, ROW_BLOCK)]

    def with_ones(v):
        return jnp.concatenate([v, jnp.ones_like(v)], axis=1)

    def exp_block(s_ref, p_ref, sl, m, width):
        p_ref[sl, :width] = jnp.concatenate(
            [jnp.exp2(s_ref[sl, j * LANES:(j + 1) * LANES] - m) for j in range(width // LANES)],
            axis=1).astype(BF16)

    start = pl.multiple_of(jnp.clip(qi * tq - WINDOW, 0, seq - span), LANES)

    def win_scores(buf):
        s_bufs[buf][:, :span] = _qk(qw_ref[...], kb_ref[pl.ds(start, span), :])

    def win_softmax(buf):
        s_ref, p_ref, m_w = s_bufs[buf], p_bufs[buf], alpha_bufs[buf]
        for sl in row_blocks:
            t = sl.start % tq
            sink = sink_ref[h * group + sl.start // tq] * LOG2E
            sb = s_ref[sl, :span] + bias_ref[t:t + ROW_BLOCK, :]
            s_ref[sl, :span] = sb
            m = jnp.maximum(jnp.max(sb, axis=-1, keepdims=True), sink)
            m_w[sl, :] = jnp.broadcast_to(m, (ROW_BLOCK, LANES))
            wden_ref[sl, :] = jnp.broadcast_to(jnp.exp2(sink - m), (ROW_BLOCK, LANES))
        for sl in row_blocks:
            exp_block(s_ref, p_ref, sl, m_w[sl, :], span)

    def win_weighted(buf):
        pv = jnp.dot(p_bufs[buf][:, :span], with_ones(vb_ref[pl.ds(start, span), :]),
                     preferred_element_type=F32)
        _unstack_heads(ob_ref, pv[:, :HEAD_DIM] / (pv[:, HEAD_DIM:] + wden_ref[...]), group, tq)

    def chunk(ref, c):
        first = c * tk if isinstance(c, int) else pl.multiple_of(c * tk, tk)
        return ref[pl.ds(first, tk), :]

    def scores(c, buf):
        s_bufs[buf][:, :tk] = _qk(qs_ref[...], chunk(ka_ref, c))

    def softmax(buf):
        s_ref, p_ref, alpha_ref = s_bufs[buf], p_bufs[buf], alpha_bufs[buf]
        for sl in row_blocks:
            m_old = m_ref[sl, :]
            m_new = jnp.maximum(m_old, jnp.max(s_ref[sl, :tk], axis=-1, keepdims=True))
            alpha_ref[sl, :] = jnp.exp2(m_old - m_new)
            m_ref[sl, :] = m_new
        for sl in row_blocks:
            exp_block(s_ref, p_ref, sl, m_ref[sl, :], tk)

    def weighted(c, buf):
        pv = jnp.dot(p_bufs[buf][:, :tk], with_ones(chunk(va_ref, c)), preferred_element_type=F32)
        for sl in row_blocks:
            alpha = alpha_bufs[buf][sl, :]
            acc_ref[sl, :] = jnp.concatenate([alpha, alpha], axis=1) * acc_ref[sl, :] + pv[sl]

    n_items = n + 1

    def stage_scores(item, buf):
        if isinstance(item, int) and item == 0:
            win_scores(buf)
        else:
            scores(item - 1, buf)

    def stage_softmax(item, buf):
        if isinstance(item, int) and item == 0:
            win_softmax(buf)
        else:
            softmax(buf)

    def stage_weighted(item, buf):
        if isinstance(item, int) and item == 0:
            win_weighted(buf)
        else:
            weighted(item - 1, buf)

    def step(t, phase):
        static = isinstance(t, int)
        if not static or 0 <= t < n_items:
            stage_softmax(t, phase)
        if not static or 0 <= t + 1 < n_items:
            stage_scores(t + 1, (phase + 1) % DEPTH)
        if not static or 0 <= t - 2 < n_items:
            stage_weighted(t - 2, (phase - 2) % DEPTH)

    for t in range(-1, n_items + 2):
        step(t, t % DEPTH)
    acc = acc_ref[...]
    _unstack_heads(oa_ref, acc[:, :HEAD_DIM] / acc[:, HEAD_DIM:], group, tq)


def _band_bias(tq, span):
    r = jnp.arange(tq)[:, None]
    c = jnp.arange(span)[None, :]
    shifts = (0, -WINDOW, tq - span)
    return jnp.stack([jnp.where(jnp.abs(c + d - r) <= WINDOW, 0.0, -jnp.inf).astype(F32) for d in shifts])


def _attention(proj, sink, batch, seq, n_kv, group, col):
    T = proj.shape[0]
    tq = _pick(seq, (256, 128))
    tk = _pick(seq, (512, 256, 128))
    if seq // tk <= 4:
        tk = _pick(seq // 2, (1024, 512, 256, 128))
    span = min(tq + 2 * WINDOW, seq)
    assert (seq // tk) % 2 == 0
    width = max(tk, span)
    nq = seq // tq
    qw = group * HEAD_DIM
    rows = group * tq
    kind = lambda i: jnp.where(i == 0, 0, jnp.where(i == nq - 1, 2, 1))
    q_spec = lambda c: pl.BlockSpec((tq, qw), lambda b, h, i: (b * nq + i, c // qw + h))
    kv_spec = lambda c: pl.BlockSpec((seq, HEAD_DIM), lambda b, h, i: (b, c // HEAD_DIM + h))
    o_spec = pl.BlockSpec((tq, qw), lambda b, h, i: (b * nq + i, h))
    o_shape = jax.ShapeDtypeStruct((T, n_kv * qw), BF16)
    return pl.pallas_call(
        functools.partial(_attn_kernel, group, tq, tk, span),
        grid=(batch, n_kv, nq),
        in_specs=[pl.BlockSpec(memory_space=pltpu.SMEM),
                  pl.BlockSpec((None, tq, span), lambda b, h, i: (kind(i), 0, 0)),
                  q_spec(col["qa"]), kv_spec(col["ka"]), kv_spec(col["va"]),
                  q_spec(col["qb"]), kv_spec(col["kb"]), kv_spec(col["vb"])],
        out_specs=[o_spec, o_spec],
        out_shape=[o_shape, o_shape],
        scratch_shapes=[_bufs(DEPTH, (rows, width), F32), _bufs(DEPTH, (rows, width), BF16),
                        _bufs(DEPTH, (rows, LANES), F32),
                        pltpu.VMEM((rows, LANES), F32), pltpu.VMEM((rows, 2 * HEAD_DIM), F32),
                        pltpu.VMEM((rows, HEAD_DIM), BF16), pltpu.VMEM((rows, HEAD_DIM), BF16),
                        pltpu.VMEM((rows, LANES), F32)],
        compiler_params=_params(("parallel", "parallel", "arbitrary")),
        name="attention",
    )(sink, _band_bias(tq, span), proj, proj, proj, proj, proj, proj)


ROW_PIECES = 2


def _merge_kernel(n_gc, gw, *refs):
    oa_ref, ob_ref = refs[0], refs[1]
    ga_refs = refs[2:2 + n_gc]
    gb_refs = refs[2 + n_gc:2 + 2 * n_gc]
    x_ref, wa_ref, wb_ref, wo_ref, gpost_ref, gffn_ref, x1_ref, h2_ref = refs[2 + 2 * n_gc:]
    rows = x_ref.shape[0] // ROW_PIECES
    for r in range(0, x_ref.shape[0], rows):
        rs = slice(r, r + rows)
        oa = oa_ref[rs, :]
        ob = ob_ref[rs, :]
        parts = []
        for c in range(n_gc):
            sl = slice(c * gw, (c + 1) * gw)
            a = jnp.dot(oa, wa_ref[:, sl], preferred_element_type=F32)
            b = jnp.dot(ob, wb_ref[:, sl], preferred_element_type=F32)
            merged = ga_refs[c][rs, :].astype(F32) * a + gb_refs[c][rs, :].astype(F32) * b
            parts.append(merged.astype(BF16))
        merged = jnp.concatenate(parts, axis=1)
        mix = jnp.dot(merged, wo_ref[...], preferred_element_type=F32)
        x1 = x_ref[rs, :] + _rms(mix, gpost_ref[...])
        x1_ref[rs, :] = x1
        h2_ref[rs, :] = _rms(x1, gffn_ref[...]).astype(BF16)


def _resident(shape):
    return pl.BlockSpec(shape, lambda i: (0, 0), pipeline_mode=pl.Buffered(1))


def _merge_out(oa, ob, proj, x, wa, wb, wo, gpost, gffn, gate_col):
    T, D = x.shape
    qa_w, qb_w = oa.shape[1], ob.shape[1]
    tm = _pick(T, (512, 256, 128))
    gw = math.gcd(gate_col, D)
    n_gc = D // gw
    gate_specs = [pl.BlockSpec((tm, gw), functools.partial(lambda i, c: (i, c), c=gate_col // gw + c))
                  for c in range(2 * n_gc)]
    vec = pl.BlockSpec((1, D), lambda i: (0, 0))
    row = lambda n: pl.BlockSpec((tm, n), lambda i: (i, 0))
    return pl.pallas_call(
        functools.partial(_merge_kernel, n_gc, gw),
        grid=(T // tm,),
        in_specs=[row(qa_w), row(qb_w), *gate_specs, row(D),
                  _resident(wa.shape), _resident(wb.shape), _resident(wo.shape), vec, vec],
        out_specs=[row(D), row(D)],
        out_shape=[jax.ShapeDtypeStruct((T, D), F32), jax.ShapeDtypeStruct((T, D), BF16)],
        compiler_params=_params(("parallel",)),
        name="merge_out",
    )(oa, ob, *([proj] * (2 * n_gc)), x, wa, wb, wo, gpost, gffn)


def _ffn_up_kernel(tm, seq_tiles, hm_ref, hp_ref, hn_ref, wa_ref, wb_ref, cwa_ref, cwb_ref,
                   cba_ref, cbb_ref, o_ref, hs_ref):
    i = pl.program_id(0)
    j = pl.program_id(1)
    halo = BF16_SUBLANES

    @pl.when(j == 0)
    def _():
        first = (i % seq_tiles) == 0
        last = (i % seq_tiles) == seq_tiles - 1
        hp = hp_ref[...]
        hn = hn_ref[...]
        hs_ref[0:halo, :] = jnp.where(first, jnp.zeros_like(hp), hp)
        hs_ref[halo:halo + tm, :] = hm_ref[...]
        hs_ref[halo + tm:, :] = jnp.where(last, jnp.zeros_like(hn), hn)

    def conv(u, cw_ref, cb_ref):
        cw = cw_ref[...]
        c = cb_ref[...] + u[halo - 1:halo - 1 + tm] * cw[0:1]
        c = c + u[halo:halo + tm] * cw[1:2]
        return c + u[halo + 1:halo + 1 + tm] * cw[2:3]

    ua = jnp.dot(hs_ref[...], wa_ref[...], preferred_element_type=F32)
    ub = jnp.dot(hs_ref[...], wb_ref[...], preferred_element_type=F32)
    a = _gelu_tanh(conv(ua, cwa_ref, cba_ref))
    o_ref[...] = (a * conv(ub, cwb_ref, cbb_ref)).astype(o_ref.dtype)


def _ffn_up(h2, w_up, conv_w, conv_b, seq):
    T, D = h2.shape
    F = w_up.shape[1] // 2
    tm = _pick(seq, (1024, 512, 256, 128))
    tn = _pick(F, (512, 256, 128))
    nb = F // tn
    halo = BF16_SUBLANES
    per = tm // halo
    n_halo_blocks = T // halo
    seq_tiles = seq // tm
    return pl.pallas_call(
        functools.partial(_ffn_up_kernel, tm, seq_tiles),
        grid=(T // tm, nb),
        in_specs=[pl.BlockSpec((tm, D), lambda i, j: (i, 0)),
                  pl.BlockSpec((halo, D), lambda i, j: (jnp.maximum(i * per - 1, 0), 0)),
                  pl.BlockSpec((halo, D), lambda i, j: (jnp.minimum((i + 1) * per, n_halo_blocks - 1), 0)),
                  pl.BlockSpec((D, tn), lambda i, j: (0, j)),
                  pl.BlockSpec((D, tn), lambda i, j: (0, nb + j)),
                  pl.BlockSpec((3, tn), lambda i, j: (0, j)),
                  pl.BlockSpec((3, tn), lambda i, j: (0, nb + j)),
                  pl.BlockSpec((1, tn), lambda i, j: (0, j)),
                  pl.BlockSpec((1, tn), lambda i, j: (0, nb + j))],
        out_specs=pl.BlockSpec((tm, tn), lambda i, j: (i, j)),
        out_shape=jax.ShapeDtypeStruct((T, F), BF16),
        scratch_shapes=[pltpu.VMEM((tm + 2 * halo, D), BF16)],
        compiler_params=_params(("parallel", "arbitrary")),
        name="ffn_up",
    )(h2, h2, h2, w_up, w_up, conv_w, conv_w, conv_b, conv_b)


def _ffn_down_kernel(g_ref, w_ref, x1_ref, gain_ref, o_ref):
    rows = o_ref.shape[0] // ROW_PIECES
    for r in range(0, o_ref.shape[0], rows):
        rs = slice(r, r + rows)
        f = jnp.dot(g_ref[rs, :], w_ref[...], preferred_element_type=F32)
        o_ref[rs, :] = x1_ref[rs, :] + _rms(f, gain_ref[...])


def _ffn_down(g, w_down, x1, gain):
    T, F = g.shape
    D = x1.shape[1]
    tm = _pick(T, (512, 256, 128))
    return pl.pallas_call(
        _ffn_down_kernel,
        grid=(T // tm,),
        in_specs=[pl.BlockSpec((tm, F), lambda i: (i, 0)), _resident(w_down.shape),
                  pl.BlockSpec((tm, D), lambda i: (i, 0)), pl.BlockSpec((1, D), lambda i: (0, 0))],
        out_specs=pl.BlockSpec((tm, D), lambda i: (i, 0)),
        out_shape=jax.ShapeDtypeStruct((T, D), F32),
        compiler_params=_params(("parallel",)),
        name="ffn_down",
    )(g, w_down, x1, gain)


def _rope_angles(pos, dim):
    inv_freq = ROPE_THETA ** (-jnp.arange(0, dim, 2, dtype=F32) / dim)
    ang = pos.astype(F32)[:, None] * inv_freq[None, :]
    return jnp.concatenate([ang, ang], axis=-1)


def _rope_tables(seq):
    t = jnp.arange(seq, dtype=jnp.int32)
    half = HEAD_DIM // 2
    ang_a = jnp.concatenate([_rope_angles(t // GRID_W, half), _rope_angles(t % GRID_W, half)], axis=-1)
    lane = jnp.arange(HEAD_DIM)[None, :]
    low = (lane % half) < (half // 2)
    cos_a, sin_a = jnp.cos(ang_a), jnp.sin(ang_a)
    sa1 = jnp.where(low, -sin_a, 0.0)
    sa2 = jnp.where(low, 0.0, sin_a)
    ang_t = _rope_angles(t, HEAD_DIM)
    sin_t = jnp.where(lane < half, -jnp.sin(ang_t), jnp.sin(ang_t))
    return cos_a, sa1, sa2, jnp.cos(ang_t), sin_t


def _trunk(x, p):
    B, S, D = x.shape
    T = B * S
    qa_w = p["wa"].shape[0]
    qb_w = p["wb"].shape[0]
    n_cols = p["w_in"].shape[1]
    kv_w = (n_cols - 2 * D - qa_w - qb_w) // 4
    n_kv = kv_w // HEAD_DIM
    group = qa_w // kv_w
    assert qb_w == qa_w
    seg = (("qa", qa_w), ("ka", kv_w), ("va", kv_w), ("qb", qb_w), ("kb", kv_w), ("vb", kv_w),
           ("gate", 2 * D))
    col = {}
    off = 0
    for kind, n in seg:
        col[kind] = off
        off += n

    x2 = x.reshape(T, D)
    proj = _in_proj(x2, p["g_pre_mix"], p["w_in"], p["gq"], p["gk"], _rope_tables(S), S, seg)
    oa, ob = _attention(proj, p["sink"], B, S, n_kv, group, col)
    x1, h2 = _merge_out(oa, ob, proj, x2, p["wa"], p["wb"], p["wo"], p["g_post_mix"], p["g_pre_ffn"],
                        col["gate"])
    g = _ffn_up(h2, p["w_up"], p["conv_w"], p["conv_b"], S)
    y = _ffn_down(g, p["w_down"], x1, p["g_post_ffn"])
    return y.reshape(B, S, D)


def kernel(x_prompt, x_sample, norm_pre_mix, w_in, q_norm_a, k_norm_a, sink_b, w_branch_a, w_branch_b,
           w_out, norm_post_mix, norm_pre_ffn, w_up, conv_w, conv_b, w_down, norm_post_ffn):
    depth = w_in.shape[0]
    for l in range(depth):
        p = dict(
            g_pre_mix=norm_pre_mix[l][None, :], w_in=w_in[l].astype(BF16),
            gq=(q_norm_a[l] * Q_SCALE)[None, :], gk=k_norm_a[l][None, :],
            sink=sink_b[l], wa=w_branch_a[l].astype(BF16), wb=w_branch_b[l].astype(BF16),
            wo=w_out[l].astype(BF16), g_post_mix=norm_post_mix[l][None, :],
            g_pre_ffn=norm_pre_ffn[l][None, :], w_up=w_up[l].astype(BF16), conv_w=conv_w[l],
            conv_b=conv_b[l][None, :], w_down=w_down[l].astype(BF16),
            g_post_ffn=norm_post_ffn[l][None, :])
        x_prompt = _trunk(x_prompt, p)
        x_sample = _trunk(x_sample, p)
    return (x_prompt, x_sample)
```

```python
import functools
import math

import jax
import jax.numpy as jnp
from jax import lax
from jax.experimental import pallas as pl
from jax.experimental.pallas import tpu as pltpu

HEAD_DIM = 128
WINDOW = 128
GRID_W = 64
ROPE_THETA = 10000.0
EPS = 1e-6
LOG2E = 1.4426950408889634
Q_SCALE = HEAD_DIM ** -0.5 * LOG2E
LANES = 128
BF16_SUBLANES = 16
VMEM_LIMIT = 56 * 1024 * 1024

F32 = jnp.float32
BF16 = jnp.bfloat16


def _pick(n, candidates):
    for c in candidates:
        if n % c == 0:
            return c
    raise ValueError(f"no tile in {candidates} divides {n}")


def _rms(x, gain):
    r = lax.rsqrt(jnp.mean(x * x, axis=-1, keepdims=True) + EPS)
    return x * r * gain


def _gelu_tanh(x):
    k = math.sqrt(2.0 / math.pi)
    half = 0.5 * x
    return half + half * jnp.tanh(x * (k + (k * 0.044715) * (x * x)))


def _params(sem):
    return pltpu.CompilerParams(dimension_semantics=sem, vmem_limit_bytes=VMEM_LIMIT)


def _bufs(count, shape, dtype):
    return [pltpu.VMEM(shape, dtype) for _ in range(count)]


PIECE_BUFS = 3


def _in_proj_kernel(chunk_kind, piece, x_ref, g_ref, w_ref, gq_ref, gk_ref, cosa_ref, sa1_ref, sa2_ref,
                    cost_ref, sint_ref, o_ref, hs_ref, acc_bufs):
    hs_ref[...] = _rms(x_ref[...], g_ref[...]).astype(BF16)
    n_pieces = o_ref.shape[1] // piece

    half, quarter = HEAD_DIM // 2, HEAD_DIM // 4

    def axial(h):
        return (h * cosa_ref[...] + pltpu.roll(h, HEAD_DIM - quarter, 1) * sa1_ref[...]
                + pltpu.roll(h, quarter, 1) * sa2_ref[...])

    def rope1d(h):
        return h * cost_ref[...] + pltpu.roll(h, half, 1) * sint_ref[...]

    def epilogue(kind, h):
        if kind == "qa":
            return axial(_rms(h, gq_ref[...]))
        if kind == "ka":
            return axial(_rms(h, gk_ref[...]))
        if kind == "qb":
            return rope1d(h) * Q_SCALE
        if kind == "kb":
            return rope1d(h)
        if kind == "gate":
            return 0.5 + 0.5 * jnp.tanh(0.5 * h)
        return h

    for t in range(n_pieces + 2):
        if t < n_pieces:
            acc_bufs[t % PIECE_BUFS][...] = jnp.dot(hs_ref[...], w_ref[:, t * piece:(t + 1) * piece],
                                                    preferred_element_type=F32)
        k = t - 2
        if k >= 0:
            acc_ref = acc_bufs[k % PIECE_BUFS]
            for c in range(piece // LANES):
                col = k * piece + c * LANES
                h = acc_ref[:, c * LANES:(c + 1) * LANES]
                o_ref[:, col:col + LANES] = epilogue(chunk_kind[col // LANES], h).astype(o_ref.dtype)


def _in_proj(x, gain, w, gq, gk, tabs, seq, seg_kinds):
    T, D = x.shape
    N = w.shape[1]
    tm = _pick(seq, (256, 128))
    piece = _pick(N, (512, 256, 128))
    chunk_kind = []
    for kind, n in seg_kinds:
        chunk_kind += [kind] * (n // LANES)
    seq_tiles = seq // tm
    tab_spec = pl.BlockSpec((tm, LANES), lambda i: (i % seq_tiles, 0))
    vec = lambda n: pl.BlockSpec((1, n), lambda i: (0, 0))
    return pl.pallas_call(
        functools.partial(_in_proj_kernel, tuple(chunk_kind), piece),
        grid=(T // tm,),
        in_specs=[pl.BlockSpec((tm, D), lambda i: (i, 0)), vec(D), _resident(w.shape), vec(LANES), vec(LANES),
                  tab_spec, tab_spec, tab_spec, tab_spec, tab_spec],
        out_specs=pl.BlockSpec((tm, N), lambda i: (i, 0)),
        out_shape=jax.ShapeDtypeStruct((T, N), BF16),
        scratch_shapes=[pltpu.VMEM((tm, D), BF16), _bufs(PIECE_BUFS, (tm, piece), F32)],
        compiler_params=_params(("parallel",)),
        name="in_proj",
    )(x, gain, w, gq, gk, *tabs)


def _unstack_heads(o_ref, o, group, tq):
    for g in range(group):
        o_ref[:, g * HEAD_DIM:(g + 1) * HEAD_DIM] = o[g * tq:(g + 1) * tq].astype(o_ref.dtype)


def _qk(q, k):
    return lax.dot_general(q, k, (((1,), (1,)), ((), ())), preferred_element_type=F32)


ROW_BLOCK = 64
DEPTH = 3


def _attn_kernel(group, tq, tk, span, sink_ref, bias_ref, qa_ref, ka_ref, va_ref, qb_ref, kb_ref, vb_ref,
                 oa_ref, ob_ref, s_bufs, p_bufs, alpha_bufs, m_ref, acc_ref, qs_ref, qw_ref, wden_ref):
    h = pl.program_id(1)
    qi = pl.program_id(2)
    seq = ka_ref.shape[0]
    rows = group * tq
    n = seq // tk
    for g in range(group):
        qs_ref[g * tq:(g + 1) * tq, :] = qa_ref[:, g * HEAD_DIM:(g + 1) * HEAD_DIM]
        qw_ref[g * tq:(g + 1) * tq, :] = qb_ref[:, g * HEAD_DIM:(g + 1) * HEAD_DIM]
    m_ref[...] = jnp.full(m_ref.shape, -jnp.inf, F32)
    acc_ref[...] = jnp.zeros(acc_ref.shape, F32)
    row_blocks = [slice(r, r + ROW_BLOCK) for r in range(0, rows, ROW_BLOCK)]

    def with_ones(v):
        return jnp.concatenate([v, jnp.ones_like(v)], axis=1)

    def exp_block(s_ref, p_ref, sl, m, width):
        p_ref[sl, :width] = jnp.concatenate(
            [jnp.exp2(s_ref[sl, j * LANES:(j + 1) * LANES] - m) for j in range(width // LANES)],
            axis=1).astype(BF16)

    start = pl.multiple_of(jnp.clip(qi * tq - WINDOW, 0, seq - span), LANES)

    def win_scores(buf):
        s_bufs[buf][:, :span] = _qk(qw_ref[...], kb_ref[pl.ds(start, span), :])

    def win_softmax(buf):
        s_ref, p_ref, m_w = s_bufs[buf], p_bufs[buf], alpha_bufs[buf]
        for sl in row_blocks:
            t = sl.start % tq
            sink = sink_ref[h * group + sl.start // tq] * LOG2E
            sb = s_ref[sl, :span] + bias_ref[t:t + ROW_BLOCK, :]
            s_ref[sl, :span] = sb
            m = jnp.maximum(jnp.max(sb, axis=-1, keepdims=True), sink)
            m_w[sl, :] = jnp.broadcast_to(m, (ROW_BLOCK, LANES))
            wden_ref[sl, :] = jnp.broadcast_to(jnp.exp2(sink - m), (ROW_BLOCK, LANES))
        for sl in row_blocks:
            exp_block(s_ref, p_ref, sl, m_w[sl, :], span)

    def win_weighted(buf):
        pv = jnp.dot(p_bufs[buf][:, :span], with_ones(vb_ref[pl.ds(start, span), :]),
                     preferred_element_type=F32)
        _unstack_heads(ob_ref, pv[:, :HEAD_DIM] / (pv[:, HEAD_DIM:] + wden_ref[...]), group, tq)

    def chunk(ref, c):
        first = c * tk if isinstance(c, int) else pl.multiple_of(c * tk, tk)
        return ref[pl.ds(first, tk), :]

    def scores(c, buf):
        s_bufs[buf][:, :tk] = _qk(qs_ref[...], chunk(ka_ref, c))

    def softmax(buf):
        s_ref, p_ref, alpha_ref = s_bufs[buf], p_bufs[buf], alpha_bufs[buf]
        for sl in row_blocks:
            m_old = m_ref[sl, :]
            m_new = jnp.maximum(m_old, jnp.max(s_ref[sl, :tk], axis=-1, keepdims=True))
            alpha_ref[sl, :] = jnp.exp2(m_old - m_new)
            m_ref[sl, :] = m_new
        for sl in row_blocks:
            exp_block(s_ref, p_ref, sl, m_ref[sl, :], tk)

    def weighted(c, buf):
        pv = jnp.dot(p_bufs[buf][:, :tk], with_ones(chunk(va_ref, c)), preferred_element_type=F32)
        for sl in row_blocks:
            alpha = alpha_bufs[buf][sl, :]
            acc_ref[sl, :] = jnp.concatenate([alpha, alpha], axis=1) * acc_ref[sl, :] + pv[sl]

    n_items = n + 1

    def stage_scores(item, buf):
        if isinstance(item, int) and item == 0:
            win_scores(buf)
        else:
            scores(item - 1, buf)

    def stage_softmax(item, buf):
        if isinstance(item, int) and item == 0:
            win_softmax(buf)
        else:
            softmax(buf)

    def stage_weighted(item, buf):
        if isinstance(item, int) and item == 0:
            win_weighted(buf)
        else:
            weighted(item - 1, buf)

    def step(t, phase):
        static = isinstance(t, int)
        if not static or 0 <= t < n_items:
            stage_softmax(t, phase)
        if not static or 0 <= t + 1 < n_items:
            stage_scores(t + 1, (phase + 1) % DEPTH)
        if not static or 0 <= t - 2 < n_items:
            stage_weighted(t - 2, (phase - 2) % DEPTH)

    for t in range(-1, n_items + 2):
        step(t, t % DEPTH)
    acc = acc_ref[...]
    _unstack_heads(oa_ref, acc[:, :HEAD_DIM] / acc[:, HEAD_DIM:], group, tq)


def _band_bias(tq, span):
    r = jnp.arange(tq)[:, None]
    c = jnp.arange(span)[None, :]
    shifts = (0, -WINDOW, tq - span)
    return jnp.stack([jnp.where(jnp.abs(c + d - r) <= WINDOW, 0.0, -jnp.inf).astype(F32) for d in shifts])


def _attention(proj, sink, batch, seq, n_kv, group, col):
    T = proj.shape[0]
    tq = _pick(seq, (256, 128))
    tk = _pick(seq, (512, 256, 128))
    if seq // tk <= 4:
        tk = _pick(seq // 2, (1024, 512, 256, 128))
    span = min(tq + 2 * WINDOW, seq)
    assert (seq // tk) % 2 == 0
    width = max(tk, span)
    nq = seq // tq
    qw = group * HEAD_DIM
    rows = group * tq
    kind = lambda i: jnp.where(i == 0, 0, jnp.where(i == nq - 1, 2, 1))
    q_spec = lambda c: pl.BlockSpec((tq, qw), lambda b, h, i: (b * nq + i, c // qw + h))
    kv_spec = lambda c: pl.BlockSpec((seq, HEAD_DIM), lambda b, h, i: (b, c // HEAD_DIM + h))
    o_spec = pl.BlockSpec((tq, qw), lambda b, h, i: (b * nq + i, h))
    o_shape = jax.ShapeDtypeStruct((T, n_kv * qw), BF16)
    return pl.pallas_call(
        functools.partial(_attn_kernel, group, tq, tk, span),
        grid=(batch, n_kv, nq),
        in_specs=[pl.BlockSpec(memory_space=pltpu.SMEM),
                  pl.BlockSpec((None, tq, span), lambda b, h, i: (kind(i), 0, 0)),
                  q_spec(col["qa"]), kv_spec(col["ka"]), kv_spec(col["va"]),
                  q_spec(col["qb"]), kv_spec(col["kb"]), kv_spec(col["vb"])],
        out_specs=[o_spec, o_spec],
        out_shape=[o_shape, o_shape],
        scratch_shapes=[_bufs(DEPTH, (rows, width), F32), _bufs(DEPTH, (rows, width), BF16),
                        _bufs(DEPTH, (rows, LANES), F32),
                        pltpu.VMEM((rows, LANES), F32), pltpu.VMEM((rows, 2 * HEAD_DIM), F32),
                        pltpu.VMEM((rows, HEAD_DIM), BF16), pltpu.VMEM((rows, HEAD_DIM), BF16),
                        pltpu.VMEM((rows, LANES), F32)],
        compiler_params=_params(("parallel", "parallel", "arbitrary")),
        name="attention",
    )(sink, _band_bias(tq, span), proj, proj, proj, proj, proj, proj)


ROW_PIECES = 2


def _merge_kernel(n_gc, gw, *refs):
    oa_ref, ob_ref = refs[0], refs[1]
    ga_refs = refs[2:2 + n_gc]
    gb_refs = refs[2 + n_gc:2 + 2 * n_gc]
    x_ref, wa_ref, wb_ref, wo_ref, gpost_ref, gffn_ref, x1_ref, h2_ref = refs[2 + 2 * n_gc:]
    rows = x_ref.shape[0] // ROW_PIECES
    for r in range(0, x_ref.shape[0], rows):
        rs = slice(r, r + rows)
        oa = oa_ref[rs, :]
        ob = ob_ref[rs, :]
        parts = []
        for c in range(n_gc):
            sl = slice(c * gw, (c + 1) * gw)
            a = jnp.dot(oa, wa_ref[:, sl], preferred_element_type=F32)
            b = jnp.dot(ob, wb_ref[:, sl], preferred_element_type=F32)
            merged = ga_refs[c][rs, :].astype(F32) * a + gb_refs[c][rs, :].astype(F32) * b
            parts.append(merged.astype(BF16))
        merged = jnp.concatenate(parts, axis=1)
        mix = jnp.dot(merged, wo_ref[...], preferred_element_type=F32)
        x1 = x_ref[rs, :] + _rms(mix, gpost_ref[...])
        x1_ref[rs, :] = x1
        h2_ref[rs, :] = _rms(x1, gffn_ref[...]).astype(BF16)


def _resident(shape):
    return pl.BlockSpec(shape, lambda i: (0, 0), pipeline_mode=pl.Buffered(1))


def _merge_out(oa, ob, proj, x, wa, wb, wo, gpost, gffn, gate_col):
    T, D = x.shape
    qa_w, qb_w = oa.shape[1], ob.shape[1]
    tm = _pick(T, (512, 256, 128))
    gw = math.gcd(gate_col, D)
    n_gc = D // gw
    gate_specs = [pl.BlockSpec((tm, gw), functools.partial(lambda i, c: (i, c), c=gate_col // gw + c))
                  for c in range(2 * n_gc)]
    vec = pl.BlockSpec((1, D), lambda i: (0, 0))
    row = lambda n: pl.BlockSpec((tm, n), lambda i: (i, 0))
    return pl.pallas_call(
        functools.partial(_merge_kernel, n_gc, gw),
        grid=(T // tm,),
        in_specs=[row(qa_w), row(qb_w), *gate_specs, row(D),
                  _resident(wa.shape), _resident(wb.shape), _resident(wo.shape), vec, vec],
        out_specs=[row(D), row(D)],
        out_shape=[jax.ShapeDtypeStruct((T, D), F32), jax.ShapeDtypeStruct((T, D), BF16)],
        compiler_params=_params(("parallel",)),
        name="merge_out",
    )(oa, ob, *([proj] * (2 * n_gc)), x, wa, wb, wo, gpost, gffn)


def _ffn_up_kernel(tm, seq_tiles, hm_ref, hp_ref, hn_ref, wa_ref, wb_ref, cwa_ref, cwb_ref,
                   cba_ref, cbb_ref, o_ref, hs_ref):
    i = pl.program_id(0)
    j = pl.program_id(1)
    halo = BF16_SUBLANES

    @pl.when(j == 0)
    def _():
        first = (i % seq_tiles) == 0
        last = (i % seq_tiles) == seq_tiles - 1
        hp = hp_ref[...]
        hn = hn_ref[...]
        hs_ref[0:halo, :] = jnp.where(first, jnp.zeros_like(hp), hp)
        hs_ref[halo:halo + tm, :] = hm_ref[...]
        hs_ref[halo + tm:, :] = jnp.where(last, jnp.zeros_like(hn), hn)

    def conv(u, cw_ref, cb_ref):
        cw = cw_ref[...]
        prev = pltpu.roll(u, 1, 0)[halo:halo + tm]
        nxt = pltpu.roll(u, u.shape[0] - 1, 0)[halo:halo + tm]
        c = cb_ref[...] + prev * cw[0:1]
        c = c + u[halo:halo + tm] * cw[1:2]
        return c + nxt * cw[2:3]

    ua = jnp.dot(hs_ref[...], wa_ref[...], preferred_element_type=F32)
    ub = jnp.dot(hs_ref[...], wb_ref[...], preferred_element_type=F32)
    a = _gelu_tanh(conv(ua, cwa_ref, cba_ref))
    o_ref[...] = (a * conv(ub, cwb_ref, cbb_ref)).astype(o_ref.dtype)


def _ffn_up(h2, w_up, conv_w, conv_b, seq):
    T, D = h2.shape
    F = w_up.shape[1] // 2
    tm = _pick(seq, (1024, 512, 256, 128))
    tn = _pick(F, (512, 256, 128))
    nb = F // tn
    halo = BF16_SUBLANES
    per = tm // halo
    n_halo_blocks = T // halo
    seq_tiles = seq // tm
    return pl.pallas_call(
        functools.partial(_ffn_up_kernel, tm, seq_tiles),
        grid=(T // tm, nb),
        in_specs=[pl.BlockSpec((tm, D), lambda i, j: (i, 0)),
                  pl.BlockSpec((halo, D), lambda i, j: (jnp.maximum(i * per - 1, 0), 0)),
                  pl.BlockSpec((halo, D), lambda i, j: (jnp.minimum((i + 1) * per, n_halo_blocks - 1), 0)),
                  pl.BlockSpec((D, tn), lambda i, j: (0, j)),
                  pl.BlockSpec((D, tn), lambda i, j: (0, nb + j)),
                  pl.BlockSpec((3, tn), lambda i, j: (0, j)),
                  pl.BlockSpec((3, tn), lambda i, j: (0, nb + j)),
                  pl.BlockSpec((1, tn), lambda i, j: (0, j)),
                  pl.BlockSpec((1, tn), lambda i, j: (0, nb + j))],
        out_specs=pl.BlockSpec((tm, tn), lambda i, j: (i, j)),
        out_shape=jax.ShapeDtypeStruct((T, F), BF16),
        scratch_shapes=[pltpu.VMEM((tm + 2 * halo, D), BF16)],
        compiler_params=_params(("parallel", "arbitrary")),
        name="ffn_up",
    )(h2, h2, h2, w_up, w_up, conv_w, conv_w, conv_b, conv_b)


def _ffn_down_kernel(g_ref, w_ref, x1_ref, gain_ref, o_ref):
    rows = o_ref.shape[0] // ROW_PIECES
    for r in range(0, o_ref.shape[0], rows):
        rs = slice(r, r + rows)
        f = jnp.dot(g_ref[rs, :], w_ref[...], preferred_element_type=F32)
        o_ref[rs, :] = x1_ref[rs, :] + _rms(f, gain_ref[...])


def _ffn_down(g, w_down, x1, gain):
    T, F = g.shape
    D = x1.shape[1]
    tm = _pick(T, (512, 256, 128))
    return pl.pallas_call(
        _ffn_down_kernel,
        grid=(T // tm,),
        in_specs=[pl.BlockSpec((tm, F), lambda i: (i, 0)), _resident(w_down.shape),
                  pl.BlockSpec((tm, D), lambda i: (i, 0)), pl.BlockSpec((1, D), lambda i: (0, 0))],
        out_specs=pl.BlockSpec((tm, D), lambda i: (i, 0)),
        out_shape=jax.ShapeDtypeStruct((T, D), F32),
        compiler_params=_params(("parallel",)),
        name="ffn_down",
    )(g, w_down, x1, gain)


def _rope_angles(pos, dim):
    inv_freq = ROPE_THETA ** (-jnp.arange(0, dim, 2, dtype=F32) / dim)
    ang = pos.astype(F32)[:, None] * inv_freq[None, :]
    return jnp.concatenate([ang, ang], axis=-1)


def _rope_tables(seq):
    t = jnp.arange(seq, dtype=jnp.int32)
    half = HEAD_DIM // 2
    ang_a = jnp.concatenate([_rope_angles(t // GRID_W, half), _rope_angles(t % GRID_W, half)], axis=-1)
    lane = jnp.arange(HEAD_DIM)[None, :]
    low = (lane % half) < (half // 2)
    cos_a, sin_a = jnp.cos(ang_a), jnp.sin(ang_a)
    sa1 = jnp.where(low, -sin_a, 0.0)
    sa2 = jnp.where(low, 0.0, sin_a)
    ang_t = _rope_angles(t, HEAD_DIM)
    sin_t = jnp.where(lane < half, -jnp.sin(ang_t), jnp.sin(ang_t))
    return cos_a, sa1, sa2, jnp.cos(ang_t), sin_t


def _trunk(x, p):
    B, S, D = x.shape
    T = B * S
    qa_w = p["wa"].shape[0]
    qb_w = p["wb"].shape[0]
    n_cols = p["w_in"].shape[1]
    kv_w = (n_cols - 2 * D - qa_w - qb_w) // 4
    n_kv = kv_w // HEAD_DIM
    group = qa_w // kv_w
    assert qb_w == qa_w
    seg = (("qa", qa_w), ("ka", kv_w), ("va", kv_w), ("qb", qb_w), ("kb", kv_w), ("vb", kv_w),
           ("gate", 2 * D))
    col = {}
    off = 0
    for kind, n in seg:
        col[kind] = off
        off += n

    x2 = x.reshape(T, D)
    proj = _in_proj(x2, p["g_pre_mix"], p["w_in"], p["gq"], p["gk"], _rope_tables(S), S, seg)
    oa, ob = _attention(proj, p["sink"], B, S, n_kv, group, col)
    x1, h2 = _merge_out(oa, ob, proj, x2, p["wa"], p["wb"], p["wo"], p["g_post_mix"], p["g_pre_ffn"],
                        col["gate"])
    g = _ffn_up(h2, p["w_up"], p["conv_w"], p["conv_b"], S)
    y = _ffn_down(g, p["w_down"], x1, p["g_post_ffn"])
    return y.reshape(B, S, D)


def kernel(x_prompt, x_sample, norm_pre_mix, w_in, q_norm_a, k_norm_a, sink_b, w_branch_a, w_branch_b,
           w_out, norm_post_mix, norm_pre_ffn, w_up, conv_w, conv_b, w_down, norm_post_ffn):
    depth = w_in.shape[0]
    for l in range(depth):
        p = dict(
            g_pre_mix=norm_pre_mix[l][None, :], w_in=w_in[l].astype(BF16),
            gq=(q_norm_a[l] * Q_SCALE)[None, :], gk=k_norm_a[l][None, :],
            sink=sink_b[l], wa=w_branch_a[l].astype(BF16), wb=w_branch_b[l].astype(BF16),
            wo=w_out[l].astype(BF16), g_post_mix=norm_post_mix[l][None, :],
            g_pre_ffn=norm_pre_ffn[l][None, :], w_up=w_up[l].astype(BF16), conv_w=conv_w[l],
            conv_b=conv_b[l][None, :], w_down=w_down[l].astype(BF16),
            g_post_ffn=norm_post_ffn[l][None, :])
        x_prompt = _trunk(x_prompt, p)
        x_sample = _trunk(x_sample, p)
    return (x_prompt, x_sample)
```

```python
import functools
import math

import jax
import jax.numpy as jnp
from jax import lax
from jax.experimental import pallas as pl
from jax.experimental.pallas import tpu as pltpu

HEAD_DIM = 128
WINDOW = 128
GRID_W = 64
ROPE_THETA = 10000.0
EPS = 1e-6
LOG2E = 1.4426950408889634
Q_SCALE = HEAD_DIM ** -0.5 * LOG2E
LANES = 128
BF16_SUBLANES = 16
VMEM_LIMIT = 56 * 1024 * 1024

F32 = jnp.float32
BF16 = jnp.bfloat16


def _pick(n, candidates):
    for c in candidates:
        if n % c == 0:
            return c
    raise ValueError(f"no tile in {candidates} divides {n}")


def _rms(x, gain):
    r = lax.rsqrt(jnp.mean(x * x, axis=-1, keepdims=True) + EPS)
    return x * r * gain


def _gelu_tanh(x):
    k = math.sqrt(2.0 / math.pi)
    half = 0.5 * x
    return half + half * jnp.tanh(x * (k + (k * 0.044715) * (x * x)))


def _params(sem):
    return pltpu.CompilerParams(dimension_semantics=sem, vmem_limit_bytes=VMEM_LIMIT)


def _bufs(count, shape, dtype):
    return [pltpu.VMEM(shape, dtype) for _ in range(count)]


PIECE_BUFS = 3


def _in_proj_kernel(chunk_kind, piece, x_ref, g_ref, w_ref, gq_ref, gk_ref, cosa_ref, sa1_ref, sa2_ref,
                    cost_ref, sint_ref, o_ref, hs_ref, acc_bufs):
    hs_ref[...] = _rms(x_ref[...], g_ref[...]).astype(BF16)
    n_pieces = o_ref.shape[1] // piece

    half, quarter = HEAD_DIM // 2, HEAD_DIM // 4

    def axial(h):
        return (h * cosa_ref[...] + pltpu.roll(h, HEAD_DIM - quarter, 1) * sa1_ref[...]
                + pltpu.roll(h, quarter, 1) * sa2_ref[...])

    def rope1d(h):
        return h * cost_ref[...] + pltpu.roll(h, half, 1) * sint_ref[...]

    def epilogue(kind, h):
        if kind == "qa":
            return axial(_rms(h, gq_ref[...]))
        if kind == "ka":
            return axial(_rms(h, gk_ref[...]))
        if kind == "qb":
            return rope1d(h) * Q_SCALE
        if kind == "kb":
            return rope1d(h)
        if kind == "gate":
            return 0.5 + 0.5 * jnp.tanh(0.5 * h)
        return h

    for t in range(n_pieces + 2):
        if t < n_pieces:
            acc_bufs[t % PIECE_BUFS][...] = jnp.dot(hs_ref[...], w_ref[:, t * piece:(t + 1) * piece],
                                                    preferred_element_type=F32)
        k = t - 2
        if k >= 0:
            acc_ref = acc_bufs[k % PIECE_BUFS]
            for c in range(piece // LANES):
                col = k * piece + c * LANES
                h = acc_ref[:, c * LANES:(c + 1) * LANES]
                o_ref[:, col:col + LANES] = epilogue(chunk_kind[col // LANES], h).astype(o_ref.dtype)


def _in_proj(x, gain, w, gq, gk, tabs, seq, seg_kinds):
    T, D = x.shape
    N = w.shape[1]
    tm = _pick(seq, (256, 128))
    piece = _pick(N, (512, 256, 128))
    chunk_kind = []
    for kind, n in seg_kinds:
        chunk_kind += [kind] * (n // LANES)
    seq_tiles = seq // tm
    tab_spec = pl.BlockSpec((tm, LANES), lambda i: (i % seq_tiles, 0))
    vec = lambda n: pl.BlockSpec((1, n), lambda i: (0, 0))
    return pl.pallas_call(
        functools.partial(_in_proj_kernel, tuple(chunk_kind), piece),
        grid=(T // tm,),
        in_specs=[pl.BlockSpec((tm, D), lambda i: (i, 0)), vec(D), _resident(w.shape), vec(LANES), vec(LANES),
                  tab_spec, tab_spec, tab_spec, tab_spec, tab_spec],
        out_specs=pl.BlockSpec((tm, N), lambda i: (i, 0)),
        out_shape=jax.ShapeDtypeStruct((T, N), BF16),
        scratch_shapes=[pltpu.VMEM((tm, D), BF16), _bufs(PIECE_BUFS, (tm, piece), F32)],
        compiler_params=_params(("parallel",)),
        name="in_proj",
    )(x, gain, w, gq, gk, *tabs)


def _unstack_heads(o_ref, o, group, tq):
    for g in range(group):
        o_ref[:, g * HEAD_DIM:(g + 1) * HEAD_DIM] = o[g * tq:(g + 1) * tq].astype(o_ref.dtype)


def _qk(q, k):
    return lax.dot_general(q, k, (((1,), (1,)), ((), ())), preferred_element_type=F32)


ROW_BLOCK = 64
DEPTH = 4


def _attn_kernel(group, tq, tk, span, sink_ref, bias_ref, qa_ref, ka_ref, va_ref, qb_ref, kb_ref, vb_ref,
                 oa_ref, ob_ref, s_bufs, p_bufs, alpha_bufs, m_ref, acc_ref, qs_ref, qw_ref, wden_ref):
    h = pl.program_id(1)
    qi = pl.program_id(2)
    seq = ka_ref.shape[0]
    rows = group * tq
    n = seq // tk
    for g in range(group):
        qs_ref[g * tq:(g + 1) * tq, :] = qa_ref[:, g * HEAD_DIM:(g + 1) * HEAD_DIM]
        qw_ref[g * tq:(g + 1) * tq, :] = qb_ref[:, g * HEAD_DIM:(g + 1) * HEAD_DIM]
    m_ref[...] = jnp.full(m_ref.shape, -jnp.inf, F32)
    acc_ref[...] = jnp.zeros(acc_ref.shape, F32)
    row_blocks = [slice(r, r + ROW_BLOCK) for r in range(0, rows, ROW_BLOCK)]

    def with_ones(v):
        return jnp.concatenate([v, jnp.ones_like(v)], axis=1)

    def exp_block(s_ref, p_ref, sl, m, width):
        p_ref[sl, :width] = jnp.concatenate(
            [jnp.exp2(s_ref[sl, j * LANES:(j + 1) * LANES] - m) for j in range(width // LANES)],
            axis=1).astype(BF16)

    start = pl.multiple_of(jnp.clip(qi * tq - WINDOW, 0, seq - span), LANES)

    def win_scores(buf):
        s_bufs[buf][:, :span] = _qk(qw_ref[...], kb_ref[pl.ds(start, span), :])

    def win_softmax(buf):
        s_ref, p_ref, m_w = s_bufs[buf], p_bufs[buf], alpha_bufs[buf]
        for sl in row_blocks:
            t = sl.start % tq
            sink = sink_ref[h * group + sl.start // tq] * LOG2E
            sb = s_ref[sl, :span] + bias_ref[t:t + ROW_BLOCK, :]
            s_ref[sl, :span] = sb
            m = jnp.maximum(jnp.max(sb, axis=-1, keepdims=True), sink)
            m_w[sl, :] = jnp.broadcast_to(m, (ROW_BLOCK, LANES))
            wden_ref[sl, :] = jnp.broadcast_to(jnp.exp2(sink - m), (ROW_BLOCK, LANES))
        for sl in row_blocks:
            exp_block(s_ref, p_ref, sl, m_w[sl, :], span)

    def win_weighted(buf):
        pv = jnp.dot(p_bufs[buf][:, :span], with_ones(vb_ref[pl.ds(start, span), :]),
                     preferred_element_type=F32)
        _unstack_heads(ob_ref, pv[:, :HEAD_DIM] / (pv[:, HEAD_DIM:] + wden_ref[...]), group, tq)

    def chunk(ref, c):
        first = c * tk if isinstance(c, int) else pl.multiple_of(c * tk, tk)
        return ref[pl.ds(first, tk), :]

    def scores(c, buf):
        s_bufs[buf][:, :tk] = _qk(qs_ref[...], chunk(ka_ref, c))

    def softmax(buf):
        s_ref, p_ref, alpha_ref = s_bufs[buf], p_bufs[buf], alpha_bufs[buf]
        for sl in row_blocks:
            m_old = m_ref[sl, :]
            m_new = jnp.maximum(m_old, jnp.max(s_ref[sl, :tk], axis=-1, keepdims=True))
            alpha_ref[sl, :] = jnp.exp2(m_old - m_new)
            m_ref[sl, :] = m_new
        for sl in row_blocks:
            exp_block(s_ref, p_ref, sl, m_ref[sl, :], tk)

    def weighted(c, buf):
        pv = jnp.dot(p_bufs[buf][:, :tk], with_ones(chunk(va_ref, c)), preferred_element_type=F32)
        for sl in row_blocks:
            alpha = alpha_bufs[buf][sl, :]
            acc_ref[sl, :] = jnp.concatenate([alpha, alpha], axis=1) * acc_ref[sl, :] + pv[sl]

    n_items = n + 1

    def stage_scores(item, buf):
        if isinstance(item, int) and item == 0:
            win_scores(buf)
        else:
            scores(item - 1, buf)

    def stage_softmax(item, buf):
        if isinstance(item, int) and item == 0:
            win_softmax(buf)
        else:
            softmax(buf)

    def stage_weighted(item, buf):
        if isinstance(item, int) and item == 0:
            win_weighted(buf)
        else:
            weighted(item - 1, buf)

    def step(t, phase):
        static = isinstance(t, int)
        if not static or 0 <= t < n_items:
            stage_softmax(t, phase)
        if not static or 0 <= t + 1 < n_items:
            stage_scores(t + 1, (phase + 1) % DEPTH)
        if not static or 0 <= t - 2 < n_items:
            stage_weighted(t - 2, (phase - 2) % DEPTH)

    for t in range(-1, n_items + 2):
        step(t, t % DEPTH)
    acc = acc_ref[...]
    _unstack_heads(oa_ref, acc[:, :HEAD_DIM] / acc[:, HEAD_DIM:], group, tq)


def _band_bias(tq, span):
    r = jnp.arange(tq)[:, None]
    c = jnp.arange(span)[None, :]
    shifts = (0, -WINDOW, tq - span)
    return jnp.stack([jnp.where(jnp.abs(c + d - r) <= WINDOW, 0.0, -jnp.inf).astype(F32) for d in shifts])


def _attention(proj, sink, batch, seq, n_kv, group, col):
    T = proj.shape[0]
    tq = _pick(seq, (256, 128))
    tk = _pick(seq, (512, 256, 128))
    if seq // tk <= 4:
        tk = _pick(seq // 2, (1024, 512, 256, 128))
    span = min(tq + 2 * WINDOW, seq)
    assert (seq // tk) % 2 == 0
    width = max(tk, span)
    nq = seq // tq
    qw = group * HEAD_DIM
    rows = group * tq
    kind = lambda i: jnp.where(i == 0, 0, jnp.where(i == nq - 1, 2, 1))
    q_spec = lambda c: pl.BlockSpec((tq, qw), lambda b, h, i: (b * nq + i, c // qw + h))
    kv_spec = lambda c: pl.BlockSpec((seq, HEAD_DIM), lambda b, h, i: (b, c // HEAD_DIM + h))
    o_spec = pl.BlockSpec((tq, qw), lambda b, h, i: (b * nq + i, h))
    o_shape = jax.ShapeDtypeStruct((T, n_kv * qw), BF16)
    return pl.pallas_call(
        functools.partial(_attn_kernel, group, tq, tk, span),
        grid=(batch, n_kv, nq),
        in_specs=[pl.BlockSpec(memory_space=pltpu.SMEM),
                  pl.BlockSpec((None, tq, span), lambda b, h, i: (kind(i), 0, 0)),
                  q_spec(col["qa"]), kv_spec(col["ka"]), kv_spec(col["va"]),
                  q_spec(col["qb"]), kv_spec(col["kb"]), kv_spec(col["vb"])],
        out_specs=[o_spec, o_spec],
        out_shape=[o_shape, o_shape],
        scratch_shapes=[_bufs(DEPTH, (rows, width), F32), _bufs(DEPTH, (rows, width), BF16),
                        _bufs(DEPTH, (rows, LANES), F32),
                        pltpu.VMEM((rows, LANES), F32), pltpu.VMEM((rows, 2 * HEAD_DIM), F32),
                        pltpu.VMEM((rows, HEAD_DIM), BF16), pltpu.VMEM((rows, HEAD_DIM), BF16),
                        pltpu.VMEM((rows, LANES), F32)],
        compiler_params=_params(("parallel", "parallel", "arbitrary")),
        name="attention",
    )(sink, _band_bias(tq, span), proj, proj, proj, proj, proj, proj)


ROW_PIECES = 2


def _merge_kernel(n_gc, gw, *refs):
    oa_ref, ob_ref = refs[0], refs[1]
    ga_refs = refs[2:2 + n_gc]
    gb_refs = refs[2 + n_gc:2 + 2 * n_gc]
    x_ref, wa_ref, wb_ref, wo_ref, gpost_ref, gffn_ref, x1_ref, h2_ref = refs[2 + 2 * n_gc:]
    rows = x_ref.shape[0] // ROW_PIECES
    for r in range(0, x_ref.shape[0], rows):
        rs = slice(r, r + rows)
        oa = oa_ref[rs, :]
        ob = ob_ref[rs, :]
        parts = []
        for c in range(n_gc):
            sl = slice(c * gw, (c + 1) * gw)
            a = jnp.dot(oa, wa_ref[:, sl], preferred_element_type=F32)
            b = jnp.dot(ob, wb_ref[:, sl], preferred_element_type=F32)
            merged = ga_refs[c][rs, :].astype(F32) * a + gb_refs[c][rs, :].astype(F32) * b
            parts.append(merged.astype(BF16))
        merged = jnp.concatenate(parts, axis=1)
        mix = jnp.dot(merged, wo_ref[...], preferred_element_type=F32)
        x1 = x_ref[rs, :] + _rms(mix, gpost_ref[...])
        x1_ref[rs, :] = x1
        h2_ref[rs, :] = _rms(x1, gffn_ref[...]).astype(BF16)


def _resident(shape):
    return pl.BlockSpec(shape, lambda i: (0, 0), pipeline_mode=pl.Buffered(1))


def _merge_out(oa, ob, proj, x, wa, wb, wo, gpost, gffn, gate_col):
    T, D = x.shape
    qa_w, qb_w = oa.shape[1], ob.shape[1]
    tm = _pick(T, (512, 256, 128))
    gw = math.gcd(gate_col, D)
    n_gc = D // gw
    gate_specs = [pl.BlockSpec((tm, gw), functools.partial(lambda i, c: (i, c), c=gate_col // gw + c))
                  for c in range(2 * n_gc)]
    vec = pl.BlockSpec((1, D), lambda i: (0, 0))
    row = lambda n: pl.BlockSpec((tm, n), lambda i: (i, 0))
    return pl.pallas_call(
        functools.partial(_merge_kernel, n_gc, gw),
        grid=(T // tm,),
        in_specs=[row(qa_w), row(qb_w), *gate_specs, row(D),
                  _resident(wa.shape), _resident(wb.shape), _resident(wo.shape), vec, vec],
        out_specs=[row(D), row(D)],
        out_shape=[jax.ShapeDtypeStruct((T, D), F32), jax.ShapeDtypeStruct((T, D), BF16)],
        compiler_params=_params(("parallel",)),
        name="merge_out",
    )(oa, ob, *([proj] * (2 * n_gc)), x, wa, wb, wo, gpost, gffn)


def _ffn_up_kernel(tm, seq_tiles, hm_ref, hp_ref, hn_ref, wa_ref, wb_ref, cwa_ref, cwb_ref,
                   cba_ref, cbb_ref, o_ref, hs_ref):
    i = pl.program_id(0)
    j = pl.program_id(1)
    halo = BF16_SUBLANES

    @pl.when(j == 0)
    def _():
        first = (i % seq_tiles) == 0
        last = (i % seq_tiles) == seq_tiles - 1
        hp = hp_ref[...]
        hn = hn_ref[...]
        hs_ref[0:halo, :] = jnp.where(first, jnp.zeros_like(hp), hp)
        hs_ref[halo:halo + tm, :] = hm_ref[...]
        hs_ref[halo + tm:, :] = jnp.where(last, jnp.zeros_like(hn), hn)

    def conv(u, cw_ref, cb_ref):
        cw = cw_ref[...]
        prev = pltpu.roll(u, 1, 0)[halo:halo + tm]
        nxt = pltpu.roll(u, u.shape[0] - 1, 0)[halo:halo + tm]
        c = cb_ref[...] + prev * cw[0:1]
        c = c + u[halo:halo + tm] * cw[1:2]
        return c + nxt * cw[2:3]

    ua = jnp.dot(hs_ref[...], wa_ref[...], preferred_element_type=F32)
    ub = jnp.dot(hs_ref[...], wb_ref[...], preferred_element_type=F32)
    a = _gelu_tanh(conv(ua, cwa_ref, cba_ref))
    o_ref[...] = (a * conv(ub, cwb_ref, cbb_ref)).astype(o_ref.dtype)


def _ffn_up(h2, w_up, conv_w, conv_b, seq):
    T, D = h2.shape
    F = w_up.shape[1] // 2
    tm = _pick(seq, (1024, 512, 256, 128))
    tn = _pick(F, (512, 256, 128))
    nb = F // tn
    halo = BF16_SUBLANES
    per = tm // halo
    n_halo_blocks = T // halo
    seq_tiles = seq // tm
    return pl.pallas_call(
        functools.partial(_ffn_up_kernel, tm, seq_tiles),
        grid=(T // tm, nb),
        in_specs=[pl.BlockSpec((tm, D), lambda i, j: (i, 0)),
                  pl.BlockSpec((halo, D), lambda i, j: (jnp.maximum(i * per - 1, 0), 0)),
                  pl.BlockSpec((halo, D), lambda i, j: (jnp.minimum((i + 1) * per, n_halo_blocks - 1), 0)),
                  pl.BlockSpec((D, tn), lambda i, j: (0, j)),
                  pl.BlockSpec((D, tn), lambda i, j: (0, nb + j)),
                  pl.BlockSpec((3, tn), lambda i, j: (0, j)),
                  pl.BlockSpec((3, tn), lambda i, j: (0, nb + j)),
                  pl.BlockSpec((1, tn), lambda i, j: (0, j)),
                  pl.BlockSpec((1, tn), lambda i, j: (0, nb + j))],
        out_specs=pl.BlockSpec((tm, tn), lambda i, j: (i, j)),
        out_shape=jax.ShapeDtypeStruct((T, F), BF16),
        scratch_shapes=[pltpu.VMEM((tm + 2 * halo, D), BF16)],
        compiler_params=_params(("parallel", "arbitrary")),
        name="ffn_up",
    )(h2, h2, h2, w_up, w_up, conv_w, conv_w, conv_b, conv_b)


def _ffn_down_kernel(g_ref, w_ref, x1_ref, gain_ref, o_ref):
    rows = o_ref.shape[0] // ROW_PIECES
    for r in range(0, o_ref.shape[0], rows):
        rs = slice(r, r + rows)
        f = jnp.dot(g_ref[rs, :], w_ref[...], preferred_element_type=F32)
        o_ref[rs, :] = x1_ref[rs, :] + _rms(f, gain_ref[...])


def _ffn_down(g, w_down, x1, gain):
    T, F = g.shape
    D = x1.shape[1]
    tm = _pick(T, (512, 256, 128))
    return pl.pallas_call(
        _ffn_down_kernel,
        grid=(T // tm,),
        in_specs=[pl.BlockSpec((tm, F), lambda i: (i, 0)), _resident(w_down.shape),
                  pl.BlockSpec((tm, D), lambda i: (i, 0)), pl.BlockSpec((1, D), lambda i: (0, 0))],
        out_specs=pl.BlockSpec((tm, D), lambda i: (i, 0)),
        out_shape=jax.ShapeDtypeStruct((T, D), F32),
        compiler_params=_params(("parallel",)),
        name="ffn_down",
    )(g, w_down, x1, gain)


def _rope_angles(pos, dim):
    inv_freq = ROPE_THETA ** (-jnp.arange(0, dim, 2, dtype=F32) / dim)
    ang = pos.astype(F32)[:, None] * inv_freq[None, :]
    return jnp.concatenate([ang, ang], axis=-1)


def _rope_tables(seq):
    t = jnp.arange(seq, dtype=jnp.int32)
    half = HEAD_DIM // 2
    ang_a = jnp.concatenate([_rope_angles(t // GRID_W, half), _rope_angles(t % GRID_W, half)], axis=-1)
    lane = jnp.arange(HEAD_DIM)[None, :]
    low = (lane % half) < (half // 2)
    cos_a, sin_a = jnp.cos(ang_a), jnp.sin(ang_a)
    sa1 = jnp.where(low, -sin_a, 0.0)
    sa2 = jnp.where(low, 0.0, sin_a)
    ang_t = _rope_angles(t, HEAD_DIM)
    sin_t = jnp.where(lane < half, -jnp.sin(ang_t), jnp.sin(ang_t))
    return cos_a, sa1, sa2, jnp.cos(ang_t), sin_t


def _trunk(x, p):
    B, S, D = x.shape
    T = B * S
    qa_w = p["wa"].shape[0]
    qb_w = p["wb"].shape[0]
    n_cols = p["w_in"].shape[1]
    kv_w = (n_cols - 2 * D - qa_w - qb_w) // 4
    n_kv = kv_w // HEAD_DIM
    group = qa_w // kv_w
    assert qb_w == qa_w
    seg = (("qa", qa_w), ("ka", kv_w), ("va", kv_w), ("qb", qb_w), ("kb", kv_w), ("vb", kv_w),
           ("gate", 2 * D))
    col = {}
    off = 0
    for kind, n in seg:
        col[kind] = off
        off += n

    x2 = x.reshape(T, D)
    proj = _in_proj(x2, p["g_pre_mix"], p["w_in"], p["gq"], p["gk"], _rope_tables(S), S, seg)
    oa, ob = _attention(proj, p["sink"], B, S, n_kv, group, col)
    x1, h2 = _merge_out(oa, ob, proj, x2, p["wa"], p["wb"], p["wo"], p["g_post_mix"], p["g_pre_ffn"],
                        col["gate"])
    g = _ffn_up(h2, p["w_up"], p["conv_w"], p["conv_b"], S)
    y = _ffn_down(g, p["w_down"], x1, p["g_post_ffn"])
    return y.reshape(B, S, D)


def kernel(x_prompt, x_sample, norm_pre_mix, w_in, q_norm_a, k_norm_a, sink_b, w_branch_a, w_branch_b,
           w_out, norm_post_mix, norm_pre_ffn, w_up, conv_w, conv_b, w_down, norm_post_ffn):
    depth = w_in.shape[0]
    for l in range(depth):
        p = dict(
            g_pre_mix=norm_pre_mix[l][None, :], w_in=w_in[l].astype(BF16),
            gq=(q_norm_a[l] * Q_SCALE)[None, :], gk=k_norm_a[l][None, :],
            sink=sink_b[l], wa=w_branch_a[l].astype(BF16), wb=w_branch_b[l].astype(BF16),
            wo=w_out[l].astype(BF16), g_post_mix=norm_post_mix[l][None, :],
            g_pre_ffn=norm_pre_ffn[l][None, :], w_up=w_up[l].astype(BF16), conv_w=conv_w[l],
            conv_b=conv_b[l][None, :], w_down=w_down[l].astype(BF16),
            g_post_ffn=norm_post_ffn[l][None, :])
        x_prompt = _trunk(x_prompt, p)
        x_sample = _trunk(x_sample, p)
    return (x_prompt, x_sample)
```
